```python
import math
import jax, jax.numpy as jnp
from jax import lax
import numpy as np

D_MODEL = 2048
BATCH = 2
SEQ = 4096
DEPTH = 2
DEC_BATCH = 128
DEC_SEQ = 4
PAST_LEN = 8192
PAGE_SIZE = 128

HGRN_HEADS = 8
HGRN_DK = 128
HGRN_DV = 128
HGRN_KWIDTH = HGRN_HEADS * HGRN_DK
HGRN_WIDTH = HGRN_HEADS * HGRN_DV
HGRN_CHUNK = 64
N_HEADS = 16
N_KV_HEADS = 4
HEAD_DIM = 64
GQ = N_HEADS // N_KV_HEADS
ATTN_WIDTH = N_HEADS * HEAD_DIM
KV_WIDTH = N_KV_HEADS * HEAD_DIM
WINDOW = 128
NUM_BUCKETS = 32
MAX_DISTANCE = 128
N_EXPERTS = 64
N_GROUPS = 8
TOPK_GROUPS = 4
TOP_K = 8
D_EXPERT = 512
ROUTED_SCALE = 2.5
EXPERT_BLOCK = 128
IN_WIDTH = 2 * HGRN_KWIDTH + 2 * HGRN_WIDTH + ATTN_WIDTH + 2 * KV_WIDTH + 2 * D_MODEL
EPS = 1e-6
NEG = -1e30
TINY = 1e-30

kernel_name = 'hgrn2_swa_sink_moe_adaln_step'


def rms_norm(x, g):
    xf = x.astype(jnp.float32)
    y = xf * lax.rsqrt(jnp.mean(xf * xf, axis=-1, keepdims=True) + EPS)
    return (y * g.astype(jnp.float32)).astype(x.dtype)


def swiglu(x, wg, wu, wd):
    return (jax.nn.silu(x @ wg) * (x @ wu)) @ wd


def t5_bias(dist, table):
    n = jnp.maximum(dist, 0)
    exact = NUM_BUCKETS // 2
    nf = jnp.maximum(n, 1).astype(jnp.float32)
    large = exact + (jnp.log(nf / exact) / math.log(MAX_DISTANCE / exact) * (NUM_BUCKETS - exact)).astype(jnp.int32)
    bucket = jnp.where(n < exact, n, jnp.clip(large, exact, NUM_BUCKETS - 1))
    b = table.astype(jnp.float32)[bucket]
    return jnp.moveaxis(b, -1, 0).reshape(N_KV_HEADS, GQ, *dist.shape)


def sink_attention(q, k, v, bias, mask, sinks):
    s = jnp.einsum('...qngd,...knd->...ngqk', q, k).astype(jnp.float32) * (HEAD_DIM ** -0.5) + bias
    s = jnp.where(mask, s, NEG)
    sink = sinks.astype(jnp.float32).reshape(N_KV_HEADS, GQ, 1, 1)
    m = jnp.maximum(jnp.max(s, axis=-1, keepdims=True), sink)
    p = jnp.exp(s - m)
    p = p / (jnp.sum(p, axis=-1, keepdims=True) + jnp.exp(sink - m))
    return jnp.einsum('...ngqk,...knd->...qngd', p.astype(v.dtype), v)


def window_attention_prompt(q, k, v, sinks, rel_bias):
    B, L = q.shape[:2]
    nb = L // WINDOW
    qb = q.reshape(B, nb, WINDOW, N_KV_HEADS, GQ, HEAD_DIM)
    pad = ((0, 0), (WINDOW, 0), (0, 0), (0, 0))
    kp, vp = jnp.pad(k, pad), jnp.pad(v, pad)
    kb = jnp.concatenate([kp[:, :-WINDOW].reshape(B, nb, WINDOW, N_KV_HEADS, HEAD_DIM),
                          k.reshape(B, nb, WINDOW, N_KV_HEADS, HEAD_DIM)], axis=2)
    vb = jnp.concatenate([vp[:, :-WINDOW].reshape(B, nb, WINDOW, N_KV_HEADS, HEAD_DIM),
                          v.reshape(B, nb, WINDOW, N_KV_HEADS, HEAD_DIM)], axis=2)
    j = jnp.arange(2 * WINDOW)
    dist = jnp.arange(WINDOW)[:, None] + WINDOW - j[None, :]
    band = (dist >= 0) & (dist < WINDOW)
    valid = (jnp.arange(nb)[:, None, None] > 0) | (j[None, None, :] >= WINDOW)
    mask = (band[None] & valid)[:, None, None]
    o = sink_attention(qb, kb, vb, t5_bias(dist, rel_bias), mask, sinks)
    return o.reshape(B, L, ATTN_WIDTH)


def window_attention_step(q, k, v, k_buf, v_buf, sinks, rel_bias):
    B, L = q.shape[:2]
    kc = jnp.concatenate([k_buf, k.astype(k_buf.dtype)], axis=1)
    vc = jnp.concatenate([v_buf, v.astype(v_buf.dtype)], axis=1)
    dist = (WINDOW + jnp.arange(L))[:, None] - jnp.arange(WINDOW + L)[None, :]
    mask = (dist >= 0) & (dist < WINDOW)
    o = sink_attention(q.reshape(B, L, N_KV_HEADS, GQ, HEAD_DIM), kc, vc, t5_bias(dist, rel_bias), mask, sinks)
    return o.reshape(B, L, ATTN_WIDTH), kc[:, L:], vc[:, L:]


def hgrn2_recurrence(q, k, v, log_f, s0):
    B, L, H, DK = q.shape
    DV = v.shape[-1]
    C = HGRN_CHUNK if L % HGRN_CHUNK == 0 else L
    nc = L // C

    def chunks(a):
        return jnp.swapaxes(a.reshape(B, nc, C, H, a.shape[-1]), 0, 1)

    causal = jnp.tril(jnp.ones((C, C), bool))[None, :, :, None, None]

    def step(S, inp):
        qc, kc, vc, gc = inp
        b = jnp.cumsum(gc, axis=1)
        decay = jnp.exp(jnp.where(causal, b[:, :, None] - b[:, None, :], NEG))
        a = jnp.einsum('bthd,bshd,btshd->bhts', qc, kc, decay)
        o = jnp.einsum('bhts,bshv->bthv', a, vc) + jnp.einsum('bthd,bhdv->bthv', qc * jnp.exp(b), S)
        b_end = b[:, -1]
        S = jnp.exp(b_end)[..., None] * S + jnp.einsum('bshd,bshv->bhdv', kc * jnp.exp(b_end[:, None] - b), vc)
        return S, o

    S, o = lax.scan(step, s0, (chunks(q), chunks(k), chunks(v), chunks(log_f)))
    return jnp.swapaxes(o, 0, 1).reshape(B, L, H, DV), S


def token_mixers(h, s0, k_buf, v_buf, lb, w_in, hgrn_onorm_g, q_norm_g, k_norm_g, sinks, rel_bias,
                 w_branch_a, w_branch_b, w_out):
    B, L, _ = h.shape
    f32 = jnp.float32
    offs = [int(o) for o in np.cumsum([HGRN_KWIDTH, HGRN_KWIDTH, HGRN_WIDTH, HGRN_WIDTH,
                                       ATTN_WIDTH, KV_WIDTH, KV_WIDTH, D_MODEL])]
    hq, hf, hv, hg, aq, ak, av, ga, gb = jnp.split(h @ w_in, offs, axis=-1)
    q_h = jax.nn.silu(hq.astype(f32)).reshape(B, L, HGRN_HEADS, HGRN_DK)
    z = hf.astype(f32).reshape(B, L, HGRN_HEADS, HGRN_DK)
    lbh = lb.reshape(HGRN_HEADS, HGRN_DK)
    f = lbh + (1.0 - lbh) * jax.nn.sigmoid(z)
    log_f = jnp.log(jnp.maximum(f, TINY))
    k_h = (1.0 - lbh) * jax.nn.sigmoid(-z)
    v_h = hv.astype(f32).reshape(B, L, HGRN_HEADS, HGRN_DV)
    o_h, s_fin = hgrn2_recurrence(q_h, k_h, v_h, log_f, s0)
    o_h = rms_norm(o_h, hgrn_onorm_g) * jax.nn.silu(hg.astype(f32).reshape(B, L, HGRN_HEADS, HGRN_DV))
    o_h = o_h.reshape(B, L, HGRN_WIDTH).astype(h.dtype)
    q_a = rms_norm(aq.reshape(B, L, N_HEADS, HEAD_DIM), q_norm_g)
    k_a = rms_norm(ak.reshape(B, L, N_KV_HEADS, HEAD_DIM), k_norm_g)
    v_a = av.reshape(B, L, N_KV_HEADS, HEAD_DIM)
    if k_buf is None:
        o_a = window_attention_prompt(q_a, k_a, v_a, sinks, rel_bias)
        k_new, v_new = k_a[:, -WINDOW:], v_a[:, -WINDOW:]
    else:
        o_a, k_new, v_new = window_attention_step(q_a, k_a, v_a, k_buf, v_buf, sinks, rel_bias)
    merged = jax.nn.sigmoid(ga) * (o_h @ w_branch_a) + jax.nn.sigmoid(gb) * (o_a.astype(h.dtype) @ w_branch_b)
    return merged @ w_out, s_fin, k_new, v_new


def route(h, w_router, router_bias):
    T = h.shape[0]
    scores = jax.nn.sigmoid((h @ w_router).astype(jnp.float32))
    biased = scores + router_bias.astype(jnp.float32)
    gscore = jnp.sum(lax.top_k(biased.reshape(T, N_GROUPS, N_EXPERTS // N_GROUPS), 2)[0], axis=-1)
    _, gidx = lax.top_k(gscore, TOPK_GROUPS)
    gmask = jnp.sum(jax.nn.one_hot(gidx, N_GROUPS, dtype=jnp.float32), axis=1) > 0
    masked = jnp.where(jnp.repeat(gmask, N_EXPERTS // N_GROUPS, axis=1), biased, NEG)
    _, ids = lax.top_k(masked, TOP_K)
    w = jnp.take_along_axis(scores, ids, axis=1)
    return ids, w / jnp.sum(w, axis=-1, keepdims=True) * ROUTED_SCALE


def routed_experts(h, ids, wts, w_gate, w_up, w_down):
    T, D = h.shape
    A = T * TOP_K
    flat_e = ids.reshape(A)
    flat_t = jnp.repeat(jnp.arange(T, dtype=jnp.int32), TOP_K)
    flat_w = wts.reshape(A)
    order = jnp.argsort(flat_e)
    se = flat_e[order]
    counts = jnp.bincount(flat_e, length=N_EXPERTS)
    padded = (counts + EXPERT_BLOCK - 1) // EXPERT_BLOCK * EXPERT_BLOCK
    pend = jnp.cumsum(padded)
    pstart = pend - padded
    cstart = jnp.cumsum(counts) - counts
    dest = pstart[se] + (jnp.arange(A) - cstart[se])
    n_blocks = -(-A // EXPERT_BLOCK) + N_EXPERTS
    R = n_blocks * EXPERT_BLOCK
    slot_tok = jnp.full((R,), T, jnp.int32).at[dest].set(flat_t[order])
    slot_w = jnp.zeros((R,), jnp.float32).at[dest].set(flat_w[order])
    block_e = jnp.minimum(jnp.searchsorted(pend, jnp.arange(n_blocks) * EXPERT_BLOCK, side='right'), N_EXPERTS - 1)
    hp = jnp.concatenate([h, jnp.zeros((1, D), h.dtype)], axis=0)

    def expert_block(args):
        tok, e = args
        return swiglu(hp[tok], w_gate[e], w_up[e], w_down[e])

    yb = lax.map(expert_block, (slot_tok.reshape(n_blocks, EXPERT_BLOCK), block_e)).reshape(R, D)
    out = jnp.zeros((T + 1, D), jnp.float32).at[slot_tok].add(yb.astype(jnp.float32) * slot_w[:, None])
    return out[:T].astype(h.dtype)


def decoder_layer(x, c, s0, k_buf, v_buf, lb, norm1_g, norm2_g, w_ada, b_ada, w_in, hgrn_onorm_g,
                  q_norm_g, k_norm_g, sinks, rel_bias, w_branch_a, w_branch_b, w_out, w_router, router_bias,
                  w_exp_gate, w_exp_up, w_exp_down, w_sh_gate, w_sh_up, w_sh_down):
    B, L, D = x.shape
    mod = (jax.nn.silu(c) @ w_ada + b_ada)[:, None, :]
    sh1, sc1, g1, sh2, sc2, g2 = jnp.split(mod, 6, axis=-1)
    h = rms_norm(x, norm1_g) * (1 + sc1) + sh1
    mix, s_fin, k_new, v_new = token_mixers(h, s0, k_buf, v_buf, lb, w_in, hgrn_onorm_g, q_norm_g, k_norm_g,
                                            sinks, rel_bias, w_branch_a, w_branch_b, w_out)
    x = x + g1 * mix
    h = (rms_norm(x, norm2_g) * (1 + sc2) + sh2).reshape(B * L, D)
    ids, wts = route(h, w_router, router_bias)
    y = swiglu(h, w_sh_gate, w_sh_up, w_sh_down) + routed_experts(h, ids, wts, w_exp_gate, w_exp_up, w_exp_down)
    x = x + g2 * y.reshape(B, L, D)
    return x, s_fin, k_new, v_new


def setup_inputs(seed: int = 0) -> dict:
    key = jax.random.key(seed)
    ks = jax.random.split(key, 32)
    f32 = jnp.float32

    def nrm(k, shape, scale=1.0):
        return jax.random.normal(k, shape, f32) * scale

    D = D_MODEL
    return {
        'x_prompt': nrm(ks[0], (BATCH, SEQ, D)),
        'x_sample': nrm(ks[1], (DEC_BATCH, DEC_SEQ, D)),
        'state_hgrn': nrm(ks[2], (DEPTH, DEC_BATCH, HGRN_HEADS, HGRN_DK, HGRN_DV), 0.5),
        'cache_k_win': nrm(ks[3], (DEPTH, DEC_BATCH, WINDOW, N_KV_HEADS, HEAD_DIM)),
        'cache_v_win': nrm(ks[4], (DEPTH, DEC_BATCH, WINDOW, N_KV_HEADS, HEAD_DIM)),
        'c_prompt': nrm(ks[5], (BATCH, D)),
        'c_sample': nrm(ks[6], (DEC_BATCH, D)),
        'norm1_g': 1.0 + nrm(ks[7], (DEPTH, D), 0.02),
        'norm2_g': 1.0 + nrm(ks[8], (DEPTH, D), 0.02),
        'w_ada': nrm(ks[9], (DEPTH, D, 6 * D), 0.5 * D ** -0.5),
        'b_ada': nrm(ks[10], (DEPTH, 6 * D), 0.02),
        'w_in': nrm(ks[11], (DEPTH, D, IN_WIDTH), D ** -0.5),
        'hgrn_lb_logits': nrm(ks[12], (DEPTH, HGRN_KWIDTH)),
        'hgrn_onorm_g': 1.0 + nrm(ks[13], (DEPTH, HGRN_DV), 0.02),
        'q_norm_g': 1.0 + nrm(ks[14], (DEPTH, HEAD_DIM), 0.02),
        'k_norm_g': 1.0 + nrm(ks[15], (DEPTH, HEAD_DIM), 0.02),
        'attn_sinks': nrm(ks[16], (DEPTH, N_HEADS)),
        'rel_bias': nrm(ks[17], (NUM_BUCKETS, N_HEADS), 0.5),
        'w_branch_a': nrm(ks[18], (DEPTH, HGRN_WIDTH, D), HGRN_WIDTH ** -0.5),
        'w_branch_b': nrm(ks[19], (DEPTH, ATTN_WIDTH, D), ATTN_WIDTH ** -0.5),
        'w_out': nrm(ks[20], (DEPTH, D, D), D ** -0.5),
        'w_router': nrm(ks[21], (DEPTH, D, N_EXPERTS), D ** -0.5),
        'router_bias': nrm(ks[22], (DEPTH, N_EXPERTS), 0.01),
        'w_exp_gate': nrm(ks[23], (DEPTH, N_EXPERTS, D, D_EXPERT), D ** -0.5),
        'w_exp_up': nrm(ks[24], (DEPTH, N_EXPERTS, D, D_EXPERT), D ** -0.5),
        'w_exp_down': nrm(ks[25], (DEPTH, N_EXPERTS, D_EXPERT, D), D_EXPERT ** -0.5),
        'w_sh_gate': nrm(ks[26], (DEPTH, D, D_EXPERT), D ** -0.5),
        'w_sh_up': nrm(ks[27], (DEPTH, D, D_EXPERT), D ** -0.5),
        'w_sh_down': nrm(ks[28], (DEPTH, D_EXPERT, D), D_EXPERT ** -0.5),
    }


def reference(x_prompt, x_sample, state_hgrn, cache_k_win, cache_v_win, c_prompt, c_sample, norm1_g, norm2_g,
              w_ada, b_ada, w_in, hgrn_lb_logits, hgrn_onorm_g, q_norm_g, k_norm_g, attn_sinks, rel_bias,
              w_branch_a, w_branch_b, w_out, w_router, router_bias, w_exp_gate, w_exp_up, w_exp_down,
              w_sh_gate, w_sh_up, w_sh_down):
    f32 = jnp.float32
    p_lb = jax.nn.softmax(hgrn_lb_logits.astype(f32), axis=0)
    lower_bounds = jnp.clip(jnp.cumsum(p_lb, axis=0) - p_lb[0], 0.0, 1.0 - 1e-6)
    yp, ys = x_prompt, x_sample
    sp_l, ss_l, kp_l, vp_l, ksl, vsl = [], [], [], [], [], []
    for l in range(DEPTH):
        lw = (lower_bounds[l], norm1_g[l], norm2_g[l], w_ada[l], b_ada[l], w_in[l], hgrn_onorm_g[l],
              q_norm_g[l], k_norm_g[l], attn_sinks[l], rel_bias, w_branch_a[l], w_branch_b[l], w_out[l],
              w_router[l], router_bias[l], w_exp_gate[l], w_exp_up[l], w_exp_down[l],
              w_sh_gate[l], w_sh_up[l], w_sh_down[l])
        s0p = jnp.zeros((x_prompt.shape[0], HGRN_HEADS, HGRN_DK, HGRN_DV), f32)
        yp, sp, kp, vp = decoder_layer(yp, c_prompt, s0p, None, None, *lw)
        ys, ss, kss, vss = decoder_layer(ys, c_sample, state_hgrn[l].astype(f32), cache_k_win[l], cache_v_win[l], *lw)
        sp_l.append(sp); ss_l.append(ss); kp_l.append(kp); vp_l.append(vp); ksl.append(kss); vsl.append(vss)
    y_prompt, y_sample = yp, ys
    state_hgrn_prompt = jnp.stack(sp_l).astype(x_prompt.dtype)
    state_hgrn_sample = jnp.stack(ss_l).astype(state_hgrn.dtype)
    k_win_prompt = jnp.stack(kp_l)
    v_win_prompt = jnp.stack(vp_l)
    k_win_sample = jnp.stack(ksl)
    v_win_sample = jnp.stack(vsl)
    return (y_prompt, y_sample, state_hgrn_prompt, state_hgrn_sample, k_win_prompt, v_win_prompt, k_win_sample, v_win_sample)
```

```python
import functools
import math

import numpy as np
import jax
import jax.numpy as jnp
from jax import lax
from jax.experimental import pallas as pl
from jax.experimental.pallas import tpu as pltpu

f32 = jnp.float32
bf16 = jnp.bfloat16
i32 = jnp.int32

D = 2048
BATCH, SEQ = 2, 4096
DEPTH = 2
DEC_BATCH, DEC_SEQ = 128, 4
HH, DK, DV = 8, 128, 128
KW = HH * DK
NH, NKV, HD = 16, 4, 64
GQ = NH // NKV
KVW = NKV * HD
WINDOW = 128
NUM_BUCKETS, MAX_DISTANCE = 32, 128
NE, NG, TOPKG, TOPK, DE = 64, 8, 4, 8, 512
EPG = NE // NG
ROUTED_SCALE = 2.5
IN_WIDTH = 2 * KW + 2 * KW + NH * HD + 2 * KVW + 2 * D
EPS = 1e-6
NEG = -1e30
TINY = 1e-30

OFF_HQ, OFF_HF, OFF_HV, OFF_HG = 0, KW, 2 * KW, 3 * KW
OFF_AQ = 4 * KW
OFF_AK = OFF_AQ + NH * HD
OFF_AV = OFF_AK + KVW
OFF_GA = OFF_AV + KVW
OFF_GB = OFF_GA + D

TP = BATCH * SEQ
TS = DEC_BATCH * DEC_SEQ
T = TP + TS
TM = 512
NPT = TP // TM
NT = T // TM
TPB = SEQ // TM
TN_IN = 512
TN_MG = 512
TMF = 128
HC = 256
HGRN_WIDTHS = (128, 64, 32, 16, 8)
HGRN_DIAG = 8
EXP_CLAMP = 80.0
SBB = 8
SPAD = 8
BLK = 256
NA = T * TOPK
NBLK = NA // BLK + NE
NROW = NBLK * BLK

V7X_VMEM_BYTES = 64 * 1024 * 1024
SUBLANES, LANES = 8, 128


def _cparams(sem, vmem_mib):
    assert vmem_mib * 1024 * 1024 < V7X_VMEM_BYTES
    return pltpu.CompilerParams(dimension_semantics=sem, vmem_limit_bytes=vmem_mib * 1024 * 1024)


def _silu(x):
    return x * jax.nn.sigmoid(x)


def _rms(x, g):
    return x * lax.rsqrt(jnp.mean(x * x, axis=-1, keepdims=True) + EPS) * g


def _dot(a, b):
    return jnp.dot(a, b, preferred_element_type=f32)


def _dot_nt(a, b):
    return lax.dot_general(a, b, (((1,), (1,)), ((), ())), preferred_element_type=f32)


def _dot_tn(a, b):
    return lax.dot_general(a, b, (((0,), (0,)), ((), ())), preferred_element_type=f32)


def _split3(x):
    a = x.astype(bf16)
    r = x - a.astype(f32)
    b = r.astype(bf16)
    c = (r - b.astype(f32)).astype(bf16)
    return a, b, c


def _mod_rows(is_sample, p_ref, s_ref, rows):
    p = p_ref[0]
    s = s_ref[...]
    if rows != DEC_BATCH:
        s = jnp.broadcast_to(s[None], (rows // DEC_BATCH, DEC_BATCH, D)).reshape(rows, D)
    return jnp.where(is_sample, s, p)


def _mod_specs(chunk, tiles_per_seq, n_prompt_tiles, extra_args=0):
    def pmap(i, *_):
        return (DEC_BATCH + jnp.minimum(i // tiles_per_seq, BATCH - 1), 0, chunk)

    def smap(i, *_):
        return (0, chunk)

    return (pl.BlockSpec((1, 1, D), pmap), pl.BlockSpec((DEC_BATCH, D), smap))


def _ada_kernel(c_ref, w_ref, b_ref, o_ref):
    c = c_ref[...]
    a = _silu(c).astype(bf16)
    o_ref[0] = _dot(a, w_ref[0].astype(bf16)) + b_ref[0]


def _ada_mod(c_all, w_ada, b_ada):
    rows = c_all.shape[0]
    tn = 1024
    return pl.pallas_call(
        _ada_kernel,
        grid=(DEPTH, 6 * D // tn),
        in_specs=[pl.BlockSpec((rows, D), lambda l, j: (0, 0)),
                  pl.BlockSpec((1, D, tn), lambda l, j: (l, 0, j)),
                  pl.BlockSpec((1, 1, tn), lambda l, j: (l, 0, j))],
        out_specs=pl.BlockSpec((1, rows, tn), lambda l, j: (l, 0, j)),
        out_shape=jax.ShapeDtypeStruct((DEPTH, rows, 6 * D), f32),
        compiler_params=_cparams(("arbitrary", "arbitrary"), 40),
        name="ada_mod",
    )(c_all, w_ada, b_ada.reshape(DEPTH, 1, 6 * D))


def _bucket_np(dist):
    n = np.maximum(dist, 0)
    exact = NUM_BUCKETS // 2
    nf = np.maximum(n, 1).astype(np.float32)
    large = exact + (np.log(nf / np.float32(exact)) / np.float32(math.log(MAX_DISTANCE / exact))
                     * np.float32(NUM_BUCKETS - exact)).astype(np.int32)
    return np.where(n < exact, n, np.clip(large, exact, NUM_BUCKETS - 1)).astype(np.int32)


def _prompt_buckets():
    j = np.arange(2 * WINDOW)
    dist = np.arange(WINDOW)[:, None] + WINDOW - j[None, :]
    band = (dist >= 0) & (dist < WINDOW)
    return np.where(band, _bucket_np(dist), -1).astype(np.int32)


def _sample_buckets():
    l = np.arange(SPAD)[:, None]
    j = np.arange(2 * WINDOW)[None, :]
    dist = WINDOW + l - j
    ok = (dist >= 0) & (dist < WINDOW) & (l < DEC_SEQ) & (j < WINDOW + DEC_SEQ)
    return np.where(ok, _bucket_np(dist), -1).astype(np.int32)


def _bias_kernel(tab_ref, bk_ref, o_ref):
    h = pl.program_id(0)
    bk = bk_ref[...]
    acc = jnp.zeros(bk.shape, f32)
    for i in range(NUM_BUCKETS):
        acc = jnp.where(bk == i, tab_ref[i, h], acc)
    o_ref[0] = jnp.where(bk < 0, NEG, acc)


def _bias_table(rel_bias, buckets):
    r, c = buckets.shape
    return pl.pallas_call(
        _bias_kernel,
        grid=(NH,),
        in_specs=[pl.BlockSpec(memory_space=pltpu.SMEM),
                  pl.BlockSpec((r, c), lambda h: (0, 0))],
        out_specs=pl.BlockSpec((1, r, c), lambda h: (h, 0, 0)),
        out_shape=jax.ShapeDtypeStruct((NH, r, c), f32),
        compiler_params=_cparams(("arbitrary",), 16),
        name="bias_table",
    )(rel_bias, jnp.asarray(buckets))


def _inproj_kernel(x_ref, g_ref, scp_ref, scs_ref, shp_ref, shs_ref, w_ref, o_ref, h_ref):
    i = pl.program_id(0)

    @pl.when(pl.program_id(1) == 0)
    def _():
        y = _rms(x_ref[...], g_ref[...])
        sc = _mod_rows(i >= NPT, scp_ref, scs_ref, TM)
        sh = _mod_rows(i >= NPT, shp_ref, shs_ref, TM)
        h_ref[...] = (y * (1.0 + sc) + sh).astype(bf16)

    o_ref[...] = _dot(h_ref[...], w_ref[...])


def _in_proj(x, norm_g, mod3, mod2, w_in_bf):
    scp, scs = _mod_specs(1, TPB, NPT)
    shp, shs = _mod_specs(0, TPB, NPT)
    return pl.pallas_call(
        _inproj_kernel,
        grid=(NT, IN_WIDTH // TN_IN),
        in_specs=[pl.BlockSpec((TM, D), lambda i, j: (i, 0)),
                  pl.BlockSpec((1, D), lambda i, j: (0, 0)),
                  scp, scs, shp, shs,
                  pl.BlockSpec((D, TN_IN), lambda i, j: (0, j))],
        out_specs=pl.BlockSpec((TM, TN_IN), lambda i, j: (i, j)),
        out_shape=jax.ShapeDtypeStruct((T, IN_WIDTH), f32),
        scratch_shapes=[pltpu.VMEM((TM, D), bf16)],
        compiler_params=_cparams(("arbitrary", "arbitrary"), 40),
        name="in_proj",
    )(x, norm_g.reshape(1, D), mod3, mod2, mod3, mod2, w_in_bf)


def _hgrn_gates(z, lb):
    f = lb + (1.0 - lb) * jax.nn.sigmoid(z)
    g = jnp.log(jnp.maximum(f, TINY))
    k = (1.0 - lb) * jax.nn.sigmoid(-z)
    return g, k


def _bcast_block_row(b, period, row):
    c, w = b.shape
    b3 = b.reshape(c // period, period, w)
    return jnp.broadcast_to(b3[:, row:row + 1, :], b3.shape).reshape(c, w)


def _hgrn_level_table():
    t = np.arange(HC)[:, None]
    s = np.arange(HC)[None, :]
    lev = np.full((HC, HC), -1, np.int32)
    for li, w in enumerate(HGRN_WIDTHS):
        m = (t // (2 * w) == s // (2 * w)) & ((t // w) % 2 == 1) & ((s // w) % 2 == 0)
        lev[m] = li
    lev[(t // HGRN_DIAG == s // HGRN_DIAG) & (s <= t)] = len(HGRN_WIDTHS)
    return lev


def _hgrn_prompt_kernel(q_ref, f_ref, v_ref, og_ref, lb_ref, ng_ref, lev_ref, o_ref, sfin_ref, st_ref):
    c = pl.program_id(1)

    @pl.when(c == 0)
    def _():
        st_ref[...] = jnp.zeros_like(st_ref)

    lb = lb_ref[...]
    g, k = _hgrn_gates(f_ref[...], lb)
    qh = _silu(q_ref[...])
    vb = v_ref[...].astype(bf16)
    r = lax.broadcasted_iota(i32, (HC, HC), 0)
    s = lax.broadcasted_iota(i32, (HC, HC), 1)
    tri = (r >= s).astype(bf16)
    g1, g2, g3 = _split3(g)
    b = _dot(tri, g1) + _dot(tri, g2) + _dot(tri, g3)
    lev = lev_ref[...]

    qs, ks = [], []
    for w in HGRN_WIDTHS:
        m = _bcast_block_row(b, 2 * w, w - 1)
        qs.append((qh * jnp.exp(jnp.minimum(b - m, 0.0))).astype(bf16))
        ks.append((k * jnp.exp(jnp.minimum(m - b, 0.0))).astype(bf16))
    m = _bcast_block_row(b, HGRN_DIAG, 0)
    qs.append((qh * jnp.exp(jnp.minimum(b - m, 0.0))).astype(bf16))
    ks.append((k * jnp.exp(jnp.minimum(m - b, EXP_CLAMP))).astype(bf16))
    b_end = b[HC - 1:HC, :]
    q_in = (qh * jnp.exp(b)).astype(bf16)
    k_out = (k * jnp.exp(b_end - b)).astype(bf16)
    e_end = jnp.exp(b_end)
    gate = _silu(og_ref[...])
    ng = ng_ref[...]

    for h in range(HH):
        sl = slice(h * DK, (h + 1) * DK)
        a = jnp.zeros((HC, HC), f32)
        for li in range(len(qs)):
            a = jnp.where(lev == li, _dot_nt(qs[li][:, sl], ks[li][:, sl]), a)
        st = st_ref[h]
        o = _dot(a.astype(bf16), vb[:, sl]) + _dot_nt(q_in[:, sl], st.astype(bf16))
        st_new = st * e_end[:, sl] + _dot_tn(vb[:, sl], k_out[:, sl])
        st_ref[h] = st_new
        o_ref[:, sl] = (_rms(o, ng) * gate[:, sl]).astype(bf16)

        @pl.when(c == pl.num_programs(1) - 1)
        def _():
            sfin_ref[0, h] = st_new.T


def _hgrn_prompt(proj, lb, onorm_g):
    nc = SEQ // HC

    def slab(k):
        return pl.BlockSpec((HC, KW), lambda b, c: (b * nc + c, k))

    return pl.pallas_call(
        _hgrn_prompt_kernel,
        grid=(BATCH, nc),
        in_specs=[slab(OFF_HQ // KW), slab(OFF_HF // KW), slab(OFF_HV // KW), slab(OFF_HG // KW),
                  pl.BlockSpec((1, KW), lambda b, c: (0, 0)),
                  pl.BlockSpec((1, DV), lambda b, c: (0, 0)),
                  pl.BlockSpec((HC, HC), lambda b, c: (0, 0))],
        out_specs=[pl.BlockSpec((HC, KW), lambda b, c: (b * nc + c, 0)),
                   pl.BlockSpec((1, HH, DK, DV), lambda b, c: (b, 0, 0, 0))],
        out_shape=[jax.ShapeDtypeStruct((TP, KW), bf16),
                   jax.ShapeDtypeStruct((BATCH, HH, DK, DV), f32)],
        scratch_shapes=[pltpu.VMEM((HH, DV, DK), f32)],
        compiler_params=_cparams(("arbitrary", "arbitrary"), 48),
        name="hgrn_prompt",
    )(proj, proj, proj, proj, lb.reshape(1, KW), onorm_g.reshape(1, DV), jnp.asarray(_hgrn_level_table()))


def _hgrn_sample_kernel(q_ref, f_ref, v_ref, og_ref, lb_ref, ng_ref, s0_ref, o_ref, snew_ref,
                        qin_ref, kout_ref, eend_ref, oacc_ref, mt_ref, q16_ref, ostage_ref):
    step = pl.program_id(0)
    L, B = DEC_SEQ, DEC_BATCH

    @pl.when(step == 0)
    def _():
        lb = lb_ref[...]
        g, k = _hgrn_gates(f_ref[...], lb)
        qh = _silu(q_ref[...])
        v = v_ref[...]
        bs = []
        acc = None
        for t in range(L):
            gt = g[t * B:(t + 1) * B]
            acc = gt if acc is None else acc + gt
            bs.append(acc)
        b_end = bs[-1]
        eend_ref[...] = jnp.exp(b_end)
        for t in range(L):
            rt = slice(t * B, (t + 1) * B)
            qin_ref[rt, :] = qh[rt] * jnp.exp(bs[t])
            kout_ref[rt, :] = k[rt] * jnp.exp(b_end - bs[t])
            for h in range(HH):
                sl = slice(h * DK, (h + 1) * DK)
                o = jnp.zeros((B, DV), f32)
                for s_ in range(t + 1):
                    rs = slice(s_ * B, (s_ + 1) * B)
                    w = jnp.sum(qh[rt, sl] * k[rs, sl] * jnp.exp(bs[t][:, sl] - bs[s_][:, sl]),
                                axis=-1, keepdims=True)
                    o = o + w * v[rs, sl]
                oacc_ref[rt, sl] = o
        mt_ref[...] = jnp.zeros_like(mt_ref)
        q16_ref[...] = jnp.zeros_like(q16_ref)

    rows = [pl.ds(pl.multiple_of(t * B + step * SBB, SBB), SBB) for t in range(L)]
    q_t = [qin_ref[rows[t], :] for t in range(L)]
    k_t = [kout_ref[rows[t], :] for t in range(L)]
    v_t = [v_ref[rows[t], :] for t in range(L)]
    e_t = eend_ref[pl.ds(pl.multiple_of(step * SBB, SBB), SBB), :]
    for bi in range(SBB):
        for h in range(HH):
            sl = slice(h * DK, (h + 1) * DK)
            s0 = s0_ref[bi, h]
            for t in range(L):
                q16_ref[t:t + 1, :] = q_t[t][bi:bi + 1, sl]
            oi = _dot(q16_ref[...].astype(bf16), s0.astype(bf16))
            for t in range(L):
                ostage_ref[t, bi:bi + 1, sl] = oi[t:t + 1]
            mt_ref[0:1, :] = e_t[bi:bi + 1, sl]
            for t in range(L):
                mt_ref[t + 1:t + 2, :] = k_t[t][bi:bi + 1, sl]
            cols = mt_ref[...].T
            sn = s0 * cols[:, 0:1]
            for t in range(L):
                sn = sn + cols[:, t + 1:t + 2] * v_t[t][bi:bi + 1, sl]
            snew_ref[bi, h] = sn
    for t in range(L):
        oacc_ref[rows[t], :] += ostage_ref[t]

    @pl.when(step == pl.num_programs(0) - 1)
    def _():
        gate = _silu(og_ref[...])
        ng = ng_ref[...]
        for h in range(HH):
            sl = slice(h * DK, (h + 1) * DK)
            o_ref[:, sl] = (_rms(oacc_ref[:, sl], ng) * gate[:, sl]).astype(bf16)


def _hgrn_sample(proj, lb, onorm_g, state):
    def slab(k):
        return pl.BlockSpec((TS, KW), lambda s: (TP // TS, k))

    return pl.pallas_call(
        _hgrn_sample_kernel,
        grid=(DEC_BATCH // SBB,),
        in_specs=[slab(OFF_HQ // KW), slab(OFF_HF // KW), slab(OFF_HV // KW), slab(OFF_HG // KW),
                  pl.BlockSpec((1, KW), lambda s: (0, 0)),
                  pl.BlockSpec((1, DV), lambda s: (0, 0)),
                  pl.BlockSpec((SBB, HH, DK, DV), lambda s: (s, 0, 0, 0))],
        out_specs=[pl.BlockSpec((TS, KW), lambda s: (0, 0)),
                   pl.BlockSpec((SBB, HH, DK, DV), lambda s: (s, 0, 0, 0))],
        out_shape=[jax.ShapeDtypeStruct((TS, KW), bf16),
                   jax.ShapeDtypeStruct((DEC_BATCH, HH, DK, DV), f32)],
        scratch_shapes=[pltpu.VMEM((TS, KW), f32), pltpu.VMEM((TS, KW), f32),
                        pltpu.VMEM((DEC_BATCH, KW), f32), pltpu.VMEM((TS, KW), f32),
                        pltpu.VMEM((LANES, DK), f32), pltpu.VMEM((2 * SUBLANES, DK), f32),
                        pltpu.VMEM((DEC_SEQ, SBB, KW), f32)],
        compiler_params=_cparams(("arbitrary",), 56),
        name="hgrn_sample",
    )(proj, proj, proj, proj, lb.reshape(1, KW), onorm_g.reshape(1, DV), state)


def _head_rms(x, g, n):
    return jnp.concatenate([_rms(x[:, i * HD:(i + 1) * HD], g) for i in range(n)], axis=1)


def _sink_softmax_pv(s, sink_col, vb):
    m = jnp.maximum(jnp.max(s, axis=-1, keepdims=True), sink_col)
    p = jnp.exp(s - m)
    den = jnp.sum(p, axis=-1, keepdims=True) + jnp.exp(sink_col - m)
    return _dot(p.astype(bf16), vb) / den


def _attn_prompt_kernel(sink_ref, q_ref, kc_ref, kp_ref, vc_ref, vp_ref, qg_ref, kg_ref, bias_ref,
                        o_ref, kn_ref):
    i = pl.program_id(1)
    kc = _head_rms(kc_ref[...], kg_ref[...], NKV)
    kp = _head_rms(kp_ref[...], kg_ref[...], NKV)
    kn_ref[...] = kc
    kk = jnp.concatenate([kp, kc], axis=0).astype(bf16)
    vv = jnp.concatenate([vp_ref[...], vc_ref[...]], axis=0).astype(bf16)
    q = q_ref[...]
    col = lax.broadcasted_iota(i32, (GQ * WINDOW, 2 * WINDOW), 1)
    valid = (i > 0) | (col >= WINDOW)
    for n in range(NKV):
        qn = jnp.concatenate(
            [_rms(q[:, (n * GQ + g) * HD:(n * GQ + g + 1) * HD], qg_ref[...]) for g in range(GQ)], axis=0)
        s = _dot_nt(qn.astype(bf16), kk[:, n * HD:(n + 1) * HD]) * (HD ** -0.5)
        s = s + bias_ref[n * GQ:(n + 1) * GQ].reshape(GQ * WINDOW, 2 * WINDOW)
        s = jnp.where(valid, s, NEG)
        sink_col = jnp.concatenate(
            [jnp.full((WINDOW, 1), sink_ref[n * GQ + g], f32) for g in range(GQ)], axis=0)
        o = _sink_softmax_pv(s, sink_col, vv[:, n * HD:(n + 1) * HD])
        for g in range(GQ):
            hh = n * GQ + g
            o_ref[:, hh * HD:(hh + 1) * HD] = o[g * WINDOW:(g + 1) * WINDOW].astype(bf16)


def _attn_prompt(proj, sinks, q_norm_g, k_norm_g, bias):
    nb = SEQ // WINDOW
    kblk, vblk = OFF_AK // KVW, OFF_AV // KVW

    def cur(col):
        return lambda b, i: (b * nb + i, col)

    def prev(col):
        return lambda b, i: (b * nb + jnp.maximum(i - 1, 0), col)

    return pl.pallas_call(
        _attn_prompt_kernel,
        grid=(BATCH, nb),
        in_specs=[pl.BlockSpec(memory_space=pltpu.SMEM),
                  pl.BlockSpec((WINDOW, NH * HD), cur(OFF_AQ // (NH * HD))),
                  pl.BlockSpec((WINDOW, KVW), cur(kblk)),
                  pl.BlockSpec((WINDOW, KVW), prev(kblk)),
                  pl.BlockSpec((WINDOW, KVW), cur(vblk)),
                  pl.BlockSpec((WINDOW, KVW), prev(vblk)),
                  pl.BlockSpec((1, HD), lambda b, i: (0, 0)),
                  pl.BlockSpec((1, HD), lambda b, i: (0, 0)),
                  pl.BlockSpec((NH, WINDOW, 2 * WINDOW), lambda b, i: (0, 0, 0))],
        out_specs=[pl.BlockSpec((WINDOW, NH * HD), lambda b, i: (b * nb + i, 0)),
                   pl.BlockSpec((WINDOW, KVW), lambda b, i: (b * nb + i, 0))],
        out_shape=[jax.ShapeDtypeStruct((TP, NH * HD), bf16),
                   jax.ShapeDtypeStruct((TP, KVW), f32)],
        compiler_params=_cparams(("arbitrary", "arbitrary"), 32),
        name="attn_prompt",
    )(sinks, proj, proj, proj, proj, proj, q_norm_g.reshape(1, HD), k_norm_g.reshape(1, HD), bias)


def _attn_sample_kernel(sink_ref, q_ref, k_ref, v_ref, ck_ref, cv_ref, qg_ref, kg_ref, bias_ref,
                        o_ref, kw_ref, vw_ref):
    pad_rows = 2 * WINDOW - WINDOW - SPAD
    for bi in range(SBB):
        kn = _head_rms(k_ref[:, bi, :], kg_ref[...], NKV)
        vn = v_ref[:, bi, :]
        ck = ck_ref[bi]
        cv = cv_ref[bi]
        kw_ref[bi, 0:WINDOW - DEC_SEQ, :] = ck[DEC_SEQ:, :]
        kw_ref[bi, WINDOW - DEC_SEQ:WINDOW, :] = kn[0:DEC_SEQ]
        vw_ref[bi, 0:WINDOW - DEC_SEQ, :] = cv[DEC_SEQ:, :]
        vw_ref[bi, WINDOW - DEC_SEQ:WINDOW, :] = vn[0:DEC_SEQ]
        zpad = jnp.zeros((pad_rows, KVW), f32)
        kk = jnp.concatenate([ck, kn, zpad], axis=0).astype(bf16)
        vv = jnp.concatenate([cv, vn, zpad], axis=0).astype(bf16)
        q = q_ref[:, bi, :]
        for n in range(NKV):
            qn = jnp.concatenate(
                [_rms(q[:, (n * GQ + g) * HD:(n * GQ + g + 1) * HD], qg_ref[...]) for g in range(GQ)], axis=0)
            s = _dot_nt(qn.astype(bf16), kk[:, n * HD:(n + 1) * HD]) * (HD ** -0.5)
            s = s + bias_ref[n * GQ:(n + 1) * GQ].reshape(GQ * SPAD, 2 * WINDOW)
            sink_col = jnp.concatenate(
                [jnp.full((SPAD, 1), sink_ref[n * GQ + g], f32) for g in range(GQ)], axis=0)
            o = _sink_softmax_pv(s, sink_col, vv[:, n * HD:(n + 1) * HD])
            for g in range(GQ):
                hh = n * GQ + g
                o_ref[:, bi, hh * HD:(hh + 1) * HD] = o[g * SPAD:(g + 1) * SPAD].astype(bf16)


def _attn_sample(q_pad, k_pad, v_pad, cache_k, cache_v, sinks, q_norm_g, k_norm_g, bias):
    return pl.pallas_call(
        _attn_sample_kernel,
        grid=(DEC_BATCH // SBB,),
        in_specs=[pl.BlockSpec(memory_space=pltpu.SMEM),
                  pl.BlockSpec((SPAD, SBB, NH * HD), lambda s: (0, s, 0)),
                  pl.BlockSpec((SPAD, SBB, KVW), lambda s: (0, s, 0)),
                  pl.BlockSpec((SPAD, SBB, KVW), lambda s: (0, s, 0)),
                  pl.BlockSpec((SBB, WINDOW, KVW), lambda s: (s, 0, 0)),
                  pl.BlockSpec((SBB, WINDOW, KVW), lambda s: (s, 0, 0)),
                  pl.BlockSpec((1, HD), lambda s: (0, 0)),
                  pl.BlockSpec((1, HD), lambda s: (0, 0)),
                  pl.BlockSpec((NH, SPAD, 2 * WINDOW), lambda s: (0, 0, 0))],
        out_specs=[pl.BlockSpec((SPAD, SBB, NH * HD), lambda s: (0, s, 0)),
                   pl.BlockSpec((SBB, WINDOW, KVW), lambda s: (s, 0, 0)),
                   pl.BlockSpec((SBB, WINDOW, KVW), lambda s: (s, 0, 0))],
        out_shape=[jax.ShapeDtypeStruct((SPAD, DEC_BATCH, NH * HD), bf16),
                   jax.ShapeDtypeStruct((DEC_BATCH, WINDOW, KVW), f32),
                   jax.ShapeDtypeStruct((DEC_BATCH, WINDOW, KVW), f32)],
        compiler_params=_cparams(("arbitrary",), 32),
        name="attn_sample",
    )(sinks, q_pad, k_pad, v_pad, cache_k, cache_v, q_norm_g.reshape(1, HD), k_norm_g.reshape(1, HD), bias)


def _merge_kernel(oh_ref, oa_ref, ga_ref, gb_ref, x_ref, g1p_ref, g1s_ref, wa_ref, wb_ref, wo_ref,
                  o_ref, acc_ref):
    i, j = pl.program_id(0), pl.program_id(1)
    merged = (jax.nn.sigmoid(ga_ref[...]) * _dot(oh_ref[...], wa_ref[...])
              + jax.nn.sigmoid(gb_ref[...]) * _dot(oa_ref[...], wb_ref[...]))
    part = _dot(merged.astype(bf16), wo_ref[...])

    @pl.when(j == 0)
    def _():
        acc_ref[...] = part

    @pl.when(j > 0)
    def _():
        acc_ref[...] += part

    @pl.when(j == pl.num_programs(1) - 1)
    def _():
        g1 = _mod_rows(i >= NPT, g1p_ref, g1s_ref, TM)
        o_ref[...] = x_ref[...] + g1 * acc_ref[...]


def _merge(o_h, o_a, proj, x, mod3, mod2, wa_bf, wb_bf, wo_bf):
    g1p, g1s = _mod_specs(2, TPB, NPT)
    return pl.pallas_call(
        _merge_kernel,
        grid=(NT, D // TN_MG),
        in_specs=[pl.BlockSpec((TM, KW), lambda i, j: (i, 0)),
                  pl.BlockSpec((TM, NH * HD), lambda i, j: (i, 0)),
                  pl.BlockSpec((TM, TN_MG), lambda i, j: (i, OFF_GA // TN_MG + j)),
                  pl.BlockSpec((TM, TN_MG), lambda i, j: (i, OFF_GB // TN_MG + j)),
                  pl.BlockSpec((TM, D), lambda i, j: (i, 0)),
                  g1p, g1s,
                  pl.BlockSpec((KW, TN_MG), lambda i, j: (0, j)),
                  pl.BlockSpec((NH * HD, TN_MG), lambda i, j: (0, j)),
                  pl.BlockSpec((TN_MG, D), lambda i, j: (j, 0))],
        out_specs=pl.BlockSpec((TM, D), lambda i, j: (i, 0)),
        out_shape=jax.ShapeDtypeStruct((T, D), f32),
        scratch_shapes=[pltpu.VMEM((TM, D), f32)],
        compiler_params=_cparams(("arbitrary", "arbitrary"), 48),
        name="merge",
    )(o_h, o_a, proj, proj, x, mod3, mod2, wa_bf, wb_bf, wo_bf)


def _first_index(hit, idx, big, axis):
    return jnp.min(jnp.where(hit, idx, big), axis=axis, keepdims=True)


def _router_kernel(x_ref, g_ref, scp_ref, scs_ref, shp_ref, shs_ref, wr_ref, rb_ref,
                   h_ref, ids_ref, wts_ref):
    i = pl.program_id(0)
    y = _rms(x_ref[...], g_ref[...])
    sc = _mod_rows(i >= NPT, scp_ref, scs_ref, TM)
    sh = _mod_rows(i >= NPT, shp_ref, shs_ref, TM)
    h = y * (1.0 + sc) + sh
    h_ref[...] = h
    wr = wr_ref[...]
    w1 = wr.astype(bf16)
    w2 = (wr - w1.astype(f32)).astype(bf16)
    h1 = h.astype(bf16)
    h2 = (h - h1.astype(f32)).astype(bf16)
    logits = _dot_nt(w1, h1) + (_dot_nt(w1, h2) + _dot_nt(w2, h1))
    scores = jax.nn.sigmoid(logits)
    biased = scores + rb_ref[...]
    b3 = biased.reshape(NG, EPG, TM)
    e_in = lax.broadcasted_iota(i32, (NG, EPG, TM), 1).astype(f32)
    m1 = jnp.max(b3, axis=1, keepdims=True)
    first = _first_index(b3 == m1, e_in, float(EPG), 1)
    m2 = jnp.max(jnp.where(e_in == first, -jnp.inf, b3), axis=1, keepdims=True)
    gs = (m1 + m2).reshape(NG, TM)
    g_idx = lax.broadcasted_iota(i32, (NG, TM), 0).astype(f32)
    gsel = jnp.zeros((NG, TM), f32)
    for _ in range(TOPKG):
        m = jnp.max(gs, axis=0, keepdims=True)
        pick = g_idx == _first_index(gs == m, g_idx, float(NG), 0)
        gsel = jnp.where(pick, 1.0, gsel)
        gs = jnp.where(pick, -jnp.inf, gs)
    emask = jnp.broadcast_to(gsel.reshape(NG, 1, TM), (NG, EPG, TM)).reshape(NE, TM)
    cur = jnp.where(emask > 0.5, biased, NEG)
    e_idx = lax.broadcasted_iota(i32, (NE, TM), 0).astype(f32)
    ids, ws = [], []
    for _ in range(TOPK):
        m = jnp.max(cur, axis=0, keepdims=True)
        first = _first_index(cur == m, e_idx, float(NE), 0)
        pick = e_idx == first
        ids.append(first)
        ws.append(jnp.sum(jnp.where(pick, scores, 0.0), axis=0, keepdims=True))
        cur = jnp.where(pick, -jnp.inf, cur)
    w = jnp.concatenate(ws, axis=0)
    ids_ref[...] = jnp.concatenate(ids, axis=0).astype(i32)
    wts_ref[...] = w / jnp.sum(w, axis=0, keepdims=True) * ROUTED_SCALE


def _router(x1, norm_g, mod3, mod2, w_router_t, router_bias):
    scp, scs = _mod_specs(4, TPB, NPT)
    shp, shs = _mod_specs(3, TPB, NPT)
    return pl.pallas_call(
        _router_kernel,
        grid=(NT,),
        in_specs=[pl.BlockSpec((TM, D), lambda i: (i, 0)),
                  pl.BlockSpec((1, D), lambda i: (0, 0)),
                  scp, scs, shp, shs,
                  pl.BlockSpec((NE, D), lambda i: (0, 0)),
                  pl.BlockSpec((NE, 1), lambda i: (0, 0))],
        out_specs=[pl.BlockSpec((TM, D), lambda i: (i, 0)),
                   pl.BlockSpec((TOPK, TM), lambda i: (0, i)),
                   pl.BlockSpec((TOPK, TM), lambda i: (0, i))],
        out_shape=[jax.ShapeDtypeStruct((T, D), f32),
                   jax.ShapeDtypeStruct((TOPK, T), i32),
                   jax.ShapeDtypeStruct((TOPK, T), f32)],
        compiler_params=_cparams(("arbitrary",), 48),
        name="router",
    )(x1, norm_g.reshape(1, D), mod3, mod2, mod3, mod2, w_router_t, router_bias.reshape(NE, 1))


def _routing_tables(ids_t):
    flat_e = ids_t.reshape(NA)
    flat_t = jnp.tile(jnp.arange(T, dtype=i32), TOPK)
    order = jnp.argsort(flat_e)
    se = flat_e[order]
    counts = jnp.bincount(flat_e, length=NE).astype(i32)
    padded = (counts + BLK - 1) // BLK * BLK
    pend = jnp.cumsum(padded)
    pstart = pend - padded
    cstart = jnp.cumsum(counts) - counts
    dest = (pstart[se] + (jnp.arange(NA, dtype=i32) - cstart[se])).astype(i32)
    slot_tok = jnp.zeros((NROW,), i32).at[dest].set(flat_t[order])
    pos = jnp.zeros((NA,), i32).at[order].set(dest)
    blk_start = jnp.arange(NBLK, dtype=i32) * BLK
    block_e = jnp.minimum(jnp.searchsorted(pend, blk_start, side='right'), NE - 1).astype(i32)
    n_used = (pend[-1] // BLK).astype(i32).reshape(1)
    return slot_tok.reshape(NBLK, 1, BLK), block_e, n_used, pos.reshape(TOPK, T)


def _expert_kernel(be_ref, nu_ref, tok_ref, tokn_ref, h_hbm, wg_ref, wu_ref, wd_ref, y_ref,
                   xbuf, sem, wgb, wub, wdb):
    i = pl.program_id(0)
    n_used = nu_ref[0]
    slot = i % 2

    def row_copy(tref, r, s):
        return pltpu.make_async_copy(h_hbm.at[pl.ds(tref[0, 0, r], 1), :],
                                     xbuf.at[s, pl.ds(r, 1), :], sem.at[s])

    def start_block(tref, s):
        def body(r, c):
            row_copy(tref, r, s).start()
            return c
        lax.fori_loop(0, BLK, body, 0, unroll=8)

    @pl.when(i == 0)
    def _():
        start_block(tok_ref, 0)

    @pl.when(i + 1 < n_used)
    def _():
        start_block(tokn_ref, 1 - slot)

    @pl.when(i < n_used)
    def _():
        def wbody(r, c):
            row_copy(tok_ref, r, slot).wait()
            return c
        lax.fori_loop(0, BLK, wbody, 0, unroll=8)

        @pl.when((i == 0) | (be_ref[i] != be_ref[jnp.maximum(i - 1, 0)]))
        def _():
            wgb[...] = wg_ref[0].astype(bf16)
            wub[...] = wu_ref[0].astype(bf16)
            wdb[...] = wd_ref[0].astype(bf16)

        x = xbuf[slot].astype(bf16)
        hm = (_silu(_dot(x, wgb[...])) * _dot(x, wub[...])).astype(bf16)
        y_ref[...] = _dot(hm, wdb[...])

    @pl.when(i >= n_used)
    def _():
        y_ref[...] = jnp.zeros_like(y_ref)


def _experts(h2, slot_tok, block_e, n_used, w_gate, w_up, w_down):
    grid_spec = pltpu.PrefetchScalarGridSpec(
        num_scalar_prefetch=2,
        grid=(NBLK,),
        in_specs=[pl.BlockSpec((1, 1, BLK), lambda i, be, nu: (i, 0, 0), memory_space=pltpu.SMEM),
                  pl.BlockSpec((1, 1, BLK), lambda i, be, nu: (jnp.minimum(i + 1, NBLK - 1), 0, 0),
                               memory_space=pltpu.SMEM),
                  pl.BlockSpec(memory_space=pl.ANY),
                  pl.BlockSpec((1, D, DE), lambda i, be, nu: (be[i], 0, 0)),
                  pl.BlockSpec((1, D, DE), lambda i, be, nu: (be[i], 0, 0)),
                  pl.BlockSpec((1, DE, D), lambda i, be, nu: (be[i], 0, 0))],
        out_specs=pl.BlockSpec((BLK, D), lambda i, be, nu: (i, 0)),
        scratch_shapes=[pltpu.VMEM((2, BLK, D), f32), pltpu.SemaphoreType.DMA((2,)),
                        pltpu.VMEM((D, DE), bf16), pltpu.VMEM((D, DE), bf16), pltpu.VMEM((DE, D), bf16)],
    )
    return pl.pallas_call(
        _expert_kernel,
        grid_spec=grid_spec,
        out_shape=jax.ShapeDtypeStruct((NROW, D), f32),
        compiler_params=_cparams(("arbitrary",), 56),
        name="experts",
    )(block_e, n_used, slot_tok, slot_tok, h2, w_gate, w_up, w_down)


def _final_kernel(pos_ref, posn_ref, x_ref, h_ref, w_ref, g2p_ref, g2s_ref, sg_ref, su_ref, sd_ref,
                  y_hbm, o_ref, ybuf, sem):
    i = pl.program_id(0)
    n = pl.num_programs(0)
    slot = i % 2

    def row_copy(pref, k, r, s):
        return pltpu.make_async_copy(y_hbm.at[pl.ds(pref[0, 0, k * TMF + r], 1), :],
                                     ybuf.at[s, k, pl.ds(r, 1), :], sem.at[s])

    def start_tile(pref, s):
        for k in range(TOPK):
            def body(r, c):
                row_copy(pref, k, r, s).start()
                return c
            lax.fori_loop(0, TMF, body, 0, unroll=8)

    @pl.when(i == 0)
    def _():
        start_tile(pos_ref, 0)

    @pl.when(i + 1 < n)
    def _():
        start_tile(posn_ref, 1 - slot)

    hb = h_ref[...].astype(bf16)
    shared = _dot((_silu(_dot(hb, sg_ref[...])) * _dot(hb, su_ref[...])).astype(bf16), sd_ref[...])

    for k in range(TOPK):
        def wbody(r, c):
            row_copy(pos_ref, k, r, slot).wait()
            return c
        lax.fori_loop(0, TMF, wbody, 0, unroll=8)

    w = w_ref[...]
    routed = w[:, 0:1] * ybuf[slot, 0]
    for k in range(1, TOPK):
        routed = routed + w[:, k:k + 1] * ybuf[slot, k]
    g2 = _mod_rows(i >= TP // TMF, g2p_ref, g2s_ref, TMF)
    o_ref[...] = x_ref[...] + g2 * (shared + routed)


def _final(x1, h2, wts, pos_tiles, mod3, mod2, sg_bf, su_bf, sd_bf, yb):
    g2p, g2s = _mod_specs(5, SEQ // TMF, TP // TMF)
    ntile = T // TMF
    return pl.pallas_call(
        _final_kernel,
        grid=(ntile,),
        in_specs=[pl.BlockSpec((1, 1, TOPK * TMF), lambda i: (i, 0, 0), memory_space=pltpu.SMEM),
                  pl.BlockSpec((1, 1, TOPK * TMF), lambda i: (jnp.minimum(i + 1, ntile - 1), 0, 0),
                               memory_space=pltpu.SMEM),
                  pl.BlockSpec((TMF, D), lambda i: (i, 0)),
                  pl.BlockSpec((TMF, D), lambda i: (i, 0)),
                  pl.BlockSpec((TMF, TOPK), lambda i: (i, 0)),
                  g2p, g2s,
                  pl.BlockSpec((D, DE), lambda i: (0, 0)),
                  pl.BlockSpec((D, DE), lambda i: (0, 0)),
                  pl.BlockSpec((DE, D), lambda i: (0, 0)),
                  pl.BlockSpec(memory_space=pl.ANY)],
        out_specs=pl.BlockSpec((TMF, D), lambda i: (i, 0)),
        out_shape=jax.ShapeDtypeStruct((T, D), f32),
        scratch_shapes=[pltpu.VMEM((2, TOPK, TMF, D), f32), pltpu.SemaphoreType.DMA((2,))],
        compiler_params=_cparams(("arbitrary",), 56),
        name="moe_combine",
    )(pos_tiles, pos_tiles, x1, h2, wts, mod3, mod2, sg_bf, su_bf, sd_bf, yb)


def _layer(x, mod, lb, state, cache_k, cache_v, bias_p, bias_s, norm1_g, norm2_g, w_in, onorm_g,
           q_norm_g, k_norm_g, sinks, w_a, w_b, w_o, w_router, router_bias,
           w_eg, w_eu, w_ed, w_sg, w_su, w_sd):
    mod3 = mod.reshape(mod.shape[0], 1, 6 * D)
    proj = _in_proj(x, norm1_g, mod3, mod, w_in.astype(bf16))

    oh_p, s_p = _hgrn_prompt(proj, lb, onorm_g)
    oh_s, s_s = _hgrn_sample(proj, lb, onorm_g, state)

    oa_p, kn_p = _attn_prompt(proj, sinks, q_norm_g, k_norm_g, bias_p)
    samp = proj[TP:].reshape(DEC_SEQ, DEC_BATCH, IN_WIDTH)
    padw = ((0, SPAD - DEC_SEQ), (0, 0), (0, 0))
    q_pad = jnp.pad(samp[:, :, OFF_AQ:OFF_AK], padw)
    k_pad = jnp.pad(samp[:, :, OFF_AK:OFF_AV], padw)
    v_pad = jnp.pad(samp[:, :, OFF_AV:OFF_GA], padw)
    oa_s, kw_s, vw_s = _attn_sample(q_pad, k_pad, v_pad, cache_k.reshape(DEC_BATCH, WINDOW, KVW),
                                    cache_v.reshape(DEC_BATCH, WINDOW, KVW), sinks, q_norm_g, k_norm_g, bias_s)

    o_h = jnp.concatenate([oh_p, oh_s], axis=0)
    o_a = jnp.concatenate([oa_p, oa_s[:DEC_SEQ].reshape(TS, NH * HD)], axis=0)
    x1 = _merge(o_h, o_a, proj, x, mod3, mod, w_a.astype(bf16), w_b.astype(bf16), w_o.astype(bf16))

    h2, ids_t, wts_t = _router(x1, norm2_g, mod3, mod, w_router.T, router_bias)
    slot_tok, block_e, n_used, pos = _routing_tables(ids_t)
    yb = _experts(h2, slot_tok, block_e, n_used, w_eg, w_eu, w_ed)
    pos_tiles = pos.reshape(TOPK, T // TMF, TMF).transpose(1, 0, 2).reshape(T // TMF, 1, TOPK * TMF)
    x2 = _final(x1, h2, wts_t.T, pos_tiles, mod3, mod, w_sg.astype(bf16), w_su.astype(bf16),
                w_sd.astype(bf16), yb)

    k_p = kn_p.reshape(BATCH, SEQ, NKV, HD)[:, SEQ - WINDOW:]
    v_p = proj[:TP, OFF_AV:OFF_GA].reshape(BATCH, SEQ, NKV, HD)[:, SEQ - WINDOW:]
    return (x2, s_p, s_s, k_p, v_p, kw_s.reshape(DEC_BATCH, WINDOW, NKV, HD),
            vw_s.reshape(DEC_BATCH, WINDOW, NKV, HD))


def kernel(x_prompt, x_sample, state_hgrn, cache_k_win, cache_v_win, c_prompt, c_sample, norm1_g, norm2_g,
           w_ada, b_ada, w_in, hgrn_lb_logits, hgrn_onorm_g, q_norm_g, k_norm_g, attn_sinks, rel_bias,
           w_branch_a, w_branch_b, w_out, w_router, router_bias, w_exp_gate, w_exp_up, w_exp_down,
           w_sh_gate, w_sh_up, w_sh_down):
    p_lb = jax.nn.softmax(hgrn_lb_logits.astype(f32), axis=0)
    lower_bounds = jnp.clip(jnp.cumsum(p_lb, axis=0) - p_lb[0], 0.0, 1.0 - 1e-6)

    c_all = jnp.concatenate([c_sample, c_prompt, jnp.zeros((SUBLANES - BATCH, D), f32)], axis=0)
    mod = _ada_mod(c_all, w_ada, b_ada)
    bias_p = _bias_table(rel_bias, _prompt_buckets())
    bias_s = _bias_table(rel_bias, _sample_buckets())

    x = jnp.concatenate([x_prompt.reshape(TP, D), x_sample.transpose(1, 0, 2).reshape(TS, D)], axis=0)
    sp_l, ss_l, kp_l, vp_l, ks_l, vs_l = [], [], [], [], [], []
    for l in range(DEPTH):
        x, s_p, s_s, k_p, v_p, k_s, v_s = _layer(
            x, mod[l], lower_bounds[l], state_hgrn[l], cache_k_win[l], cache_v_win[l], bias_p, bias_s,
            norm1_g[l], norm2_g[l], w_in[l], hgrn_onorm_g[l], q_norm_g[l], k_norm_g[l], attn_sinks[l],
            w_branch_a[l], w_branch_b[l], w_out[l], w_router[l], router_bias[l],
            w_exp_gate[l], w_exp_up[l], w_exp_down[l], w_sh_gate[l], w_sh_up[l], w_sh_down[l])
        sp_l.append(s_p); ss_l.append(s_s); kp_l.append(k_p); vp_l.append(v_p); ks_l.append(k_s); vs_l.append(v_s)

    y_prompt = x[:TP].reshape(BATCH, SEQ, D)
    y_sample = x[TP:].reshape(DEC_SEQ, DEC_BATCH, D).transpose(1, 0, 2)
    return (y_prompt, y_sample, jnp.stack(sp_l), jnp.stack(ss_l), jnp.stack(kp_l), jnp.stack(vp_l),
            jnp.stack(ks_l), jnp.stack(vs_l))
```

```python
import functools
import math

import numpy as np
import jax
import jax.numpy as jnp
from jax import lax
from jax.experimental import pallas as pl
from jax.experimental.pallas import tpu as pltpu

f32 = jnp.float32
bf16 = jnp.bfloat16
i32 = jnp.int32
u32 = jnp.uint32

D = 2048
BATCH, SEQ = 2, 4096
DEPTH = 2
DEC_BATCH, DEC_SEQ = 128, 4
HH, DK, DV = 8, 128, 128
KW = HH * DK
NH, NKV, HD = 16, 4, 64
GQ = NH // NKV
KVW = NKV * HD
WINDOW = 128
NUM_BUCKETS, MAX_DISTANCE = 32, 128
NE, NG, TOPKG, TOPK, DE = 64, 8, 4, 8, 512
EPG = NE // NG
ROUTED_SCALE = 2.5
IN_WIDTH = 2 * KW + 2 * KW + NH * HD + 2 * KVW + 2 * D
EPS = 1e-6
NEG = -1e30
TINY = 1e-30

OFF_HQ, OFF_HF, OFF_HV, OFF_HG = 0, KW, 2 * KW, 3 * KW
OFF_AQ = 4 * KW
OFF_AK = OFF_AQ + NH * HD
OFF_AV = OFF_AK + KVW
OFF_GA = OFF_AV + KVW
OFF_GB = OFF_GA + D

TP = BATCH * SEQ
TS = DEC_BATCH * DEC_SEQ
T = TP + TS
TM = 512
NPT = TP // TM
NT = T // TM
TPB = SEQ // TM
TN_IN = 2432
TN_MG = 512
TMF = 128
HC = 256
HGRN_WIDTHS = (128, 64, 32, 16, 8)
HGRN_DIAG = 8
EXP_CLAMP = 80.0
SBB = 8
SPAD = 8
BLK = 256
NA = T * TOPK
NBLK = NA // BLK + NE
NROW = NBLK * BLK

V7X_VMEM_BYTES = 64 * 1024 * 1024
SUBLANES, LANES = 8, 128


def _cparams(sem, vmem_mib):
    assert vmem_mib * 1024 * 1024 < V7X_VMEM_BYTES
    return pltpu.CompilerParams(dimension_semantics=sem, vmem_limit_bytes=vmem_mib * 1024 * 1024)


def _silu(x):
    return x * jax.nn.sigmoid(x)


def _rms(x, g):
    return x * lax.rsqrt(jnp.mean(x * x, axis=-1, keepdims=True) + EPS) * g


def _dot(a, b):
    return jnp.dot(a, b, preferred_element_type=f32)


def _dot_nt(a, b):
    return lax.dot_general(a, b, (((1,), (1,)), ((), ())), preferred_element_type=f32)


def _dot_tn(a, b):
    return lax.dot_general(a, b, (((0,), (0,)), ((), ())), preferred_element_type=f32)


def _split3(x):
    a = x.astype(bf16)
    r = x - a.astype(f32)
    b = r.astype(bf16)
    c = (r - b.astype(f32)).astype(bf16)
    return a, b, c


def _mod_rows(is_sample, p_ref, s_ref, rows):
    p = p_ref[0]
    s = s_ref[...]
    if rows != DEC_BATCH:
        s = jnp.broadcast_to(s[None], (rows // DEC_BATCH, DEC_BATCH, D)).reshape(rows, D)
    return jnp.where(is_sample, s, p)


def _mod_specs(chunk, tiles_per_seq, n_prompt_tiles, extra_args=0):
    def pmap(i, *_):
        return (DEC_BATCH + jnp.minimum(i // tiles_per_seq, BATCH - 1), 0, chunk)

    def smap(i, *_):
        return (0, chunk)

    return (pl.BlockSpec((1, 1, D), pmap), pl.BlockSpec((DEC_BATCH, D), smap))


def _ada_kernel(c_ref, w_ref, b_ref, o_ref):
    c = c_ref[...]
    a = _silu(c).astype(bf16)
    o_ref[0] = _dot(a, w_ref[0].astype(bf16)) + b_ref[0]


def _ada_mod(c_all, w_ada, b_ada):
    rows = c_all.shape[0]
    tn = 1024
    return pl.pallas_call(
        _ada_kernel,
        grid=(DEPTH, 6 * D // tn),
        in_specs=[pl.BlockSpec((rows, D), lambda l, j: (0, 0)),
                  pl.BlockSpec((1, D, tn), lambda l, j: (l, 0, j)),
                  pl.BlockSpec((1, 1, tn), lambda l, j: (l, 0, j))],
        out_specs=pl.BlockSpec((1, rows, tn), lambda l, j: (l, 0, j)),
        out_shape=jax.ShapeDtypeStruct((DEPTH, rows, 6 * D), f32),
        compiler_params=_cparams(("arbitrary", "arbitrary"), 40),
        name="ada_mod",
    )(c_all, w_ada, b_ada.reshape(DEPTH, 1, 6 * D))


def _bucket_np(dist):
    n = np.maximum(dist, 0)
    exact = NUM_BUCKETS // 2
    nf = np.maximum(n, 1).astype(np.float32)
    large = exact + (np.log(nf / np.float32(exact)) / np.float32(math.log(MAX_DISTANCE / exact))
                     * np.float32(NUM_BUCKETS - exact)).astype(np.int32)
    return np.where(n < exact, n, np.clip(large, exact, NUM_BUCKETS - 1)).astype(np.int32)


def _prompt_buckets():
    j = np.arange(2 * WINDOW)
    dist = np.arange(WINDOW)[:, None] + WINDOW - j[None, :]
    band = (dist >= 0) & (dist < WINDOW)
    return np.where(band, _bucket_np(dist), -1).astype(np.int32)


def _sample_buckets():
    l = np.arange(SPAD)[:, None]
    j = np.arange(2 * WINDOW)[None, :]
    dist = WINDOW + l - j
    ok = (dist >= 0) & (dist < WINDOW) & (l < DEC_SEQ) & (j < WINDOW + DEC_SEQ)
    return np.where(ok, _bucket_np(dist), -1).astype(np.int32)


def _bias_kernel(tab_ref, bk_ref, o_ref):
    h = pl.program_id(0)
    bk = bk_ref[...]
    acc = jnp.zeros(bk.shape, f32)
    for i in range(NUM_BUCKETS):
        acc = jnp.where(bk == i, tab_ref[i, h], acc)
    o_ref[0] = jnp.where(bk < 0, NEG, acc)


def _bias_table(rel_bias, buckets):
    r, c = buckets.shape
    return pl.pallas_call(
        _bias_kernel,
        grid=(NH,),
        in_specs=[pl.BlockSpec(memory_space=pltpu.SMEM),
                  pl.BlockSpec((r, c), lambda h: (0, 0))],
        out_specs=pl.BlockSpec((1, r, c), lambda h: (h, 0, 0)),
        out_shape=jax.ShapeDtypeStruct((NH, r, c), f32),
        compiler_params=_cparams(("arbitrary",), 16),
        name="bias_table",
    )(rel_bias, jnp.asarray(buckets))


def _inproj_kernel(x_ref, g_ref, scp_ref, scs_ref, shp_ref, shs_ref, w_ref, o_ref, h_ref):
    i = pl.program_id(0)

    @pl.when(pl.program_id(1) == 0)
    def _():
        y = _rms(x_ref[...], g_ref[...])
        sc = _mod_rows(i >= NPT, scp_ref, scs_ref, TM)
        sh = _mod_rows(i >= NPT, shp_ref, shs_ref, TM)
        h_ref[...] = (y * (1.0 + sc) + sh).astype(bf16)

    o_ref[...] = _dot(h_ref[...], w_ref[0])


def _in_proj(l, x, norm_g, mod3, mod2, w_in_bf):
    scp, scs = _mod_specs(1, TPB, NPT)
    shp, shs = _mod_specs(0, TPB, NPT)
    return pl.pallas_call(
        _inproj_kernel,
        grid=(NT, IN_WIDTH // TN_IN),
        in_specs=[pl.BlockSpec((TM, D), lambda i, j: (i, 0)),
                  pl.BlockSpec((1, D), lambda i, j: (0, 0)),
                  scp, scs, shp, shs,
                  pl.BlockSpec((1, D, TN_IN), lambda i, j: (l, 0, j))],
        out_specs=pl.BlockSpec((TM, TN_IN), lambda i, j: (i, j)),
        out_shape=jax.ShapeDtypeStruct((T, IN_WIDTH), f32),
        scratch_shapes=[pltpu.VMEM((TM, D), bf16)],
        compiler_params=_cparams(("arbitrary", "arbitrary"), 56),
        name="in_proj",
    )(x, norm_g.reshape(1, D), mod3, mod2, mod3, mod2, w_in_bf)


def _hgrn_gates(z, lb):
    f = lb + (1.0 - lb) * jax.nn.sigmoid(z)
    g = jnp.log(jnp.maximum(f, TINY))
    k = (1.0 - lb) * jax.nn.sigmoid(-z)
    return g, k


def _bcast_block_row(b, period, row):
    c, w = b.shape
    b3 = b.reshape(c // period, period, w)
    return jnp.broadcast_to(b3[:, row:row + 1, :], b3.shape).reshape(c, w)


def _hgrn_level_table():
    t = np.arange(HC)[:, None]
    s = np.arange(HC)[None, :]
    lev = np.full((HC, HC), -1, np.int32)
    for li, w in enumerate(HGRN_WIDTHS):
        m = (t // (2 * w) == s // (2 * w)) & ((t // w) % 2 == 1) & ((s // w) % 2 == 0)
        lev[m] = li
    lev[(t // HGRN_DIAG == s // HGRN_DIAG) & (s <= t)] = len(HGRN_WIDTHS)
    return lev


def _hgrn_prompt_kernel(q_ref, f_ref, v_ref, og_ref, lb_ref, ng_ref, lev_ref, o_ref, sfin_ref, st_ref):
    c = pl.program_id(1)

    @pl.when(c == 0)
    def _():
        st_ref[...] = jnp.zeros_like(st_ref)

    lb = lb_ref[...]
    g, k = _hgrn_gates(f_ref[...], lb)
    qh = _silu(q_ref[...])
    vb = v_ref[...].astype(bf16)
    r = lax.broadcasted_iota(i32, (HC, HC), 0)
    s = lax.broadcasted_iota(i32, (HC, HC), 1)
    tri = (r >= s).astype(bf16)
    g1, g2, g3 = _split3(g)
    b = _dot(tri, g1) + _dot(tri, g2) + _dot(tri, g3)
    lev = lev_ref[...]

    qs, ks = [], []
    for w in HGRN_WIDTHS:
        m = _bcast_block_row(b, 2 * w, w - 1)
        qs.append((qh * jnp.exp(jnp.minimum(b - m, 0.0))).astype(bf16))
        ks.append((k * jnp.exp(jnp.minimum(m - b, 0.0))).astype(bf16))
    m = _bcast_block_row(b, HGRN_DIAG, 0)
    qs.append((qh * jnp.exp(jnp.minimum(b - m, 0.0))).astype(bf16))
    ks.append((k * jnp.exp(jnp.minimum(m - b, EXP_CLAMP))).astype(bf16))
    b_end = b[HC - 1:HC, :]
    q_in = (qh * jnp.exp(b)).astype(bf16)
    k_out = (k * jnp.exp(b_end - b)).astype(bf16)
    e_end = jnp.exp(b_end)
    gate = _silu(og_ref[...])
    ng = ng_ref[...]

    for h in range(HH):
        sl = slice(h * DK, (h + 1) * DK)
        a = jnp.zeros((HC, HC), f32)
        for li in range(len(qs)):
            a = jnp.where(lev == li, _dot_nt(qs[li][:, sl], ks[li][:, sl]), a)
        st = st_ref[h]
        o = _dot(a.astype(bf16), vb[:, sl]) + _dot_nt(q_in[:, sl], st.astype(bf16))
        st_new = st * e_end[:, sl] + _dot_tn(vb[:, sl], k_out[:, sl])
        st_ref[h] = st_new
        o_ref[:, sl] = (_rms(o, ng) * gate[:, sl]).astype(bf16)

        @pl.when(c == pl.num_programs(1) - 1)
        def _():
            sfin_ref[0, h] = st_new.T


def _hgrn_prompt(proj, lb, onorm_g):
    nc = SEQ // HC

    def slab(k):
        return pl.BlockSpec((HC, KW), lambda b, c: (b * nc + c, k))

    return pl.pallas_call(
        _hgrn_prompt_kernel,
        grid=(BATCH, nc),
        in_specs=[slab(OFF_HQ // KW), slab(OFF_HF // KW), slab(OFF_HV // KW), slab(OFF_HG // KW),
                  pl.BlockSpec((1, KW), lambda b, c: (0, 0)),
                  pl.BlockSpec((1, DV), lambda b, c: (0, 0)),
                  pl.BlockSpec((HC, HC), lambda b, c: (0, 0))],
        out_specs=[pl.BlockSpec((HC, KW), lambda b, c: (b * nc + c, 0)),
                   pl.BlockSpec((1, HH, DK, DV), lambda b, c: (b, 0, 0, 0))],
        out_shape=[jax.ShapeDtypeStruct((TP, KW), bf16),
                   jax.ShapeDtypeStruct((BATCH, HH, DK, DV), f32)],
        scratch_shapes=[pltpu.VMEM((HH, DV, DK), f32)],
        compiler_params=_cparams(("arbitrary", "arbitrary"), 48),
        name="hgrn_prompt",
    )(proj, proj, proj, proj, lb.reshape(1, KW), onorm_g.reshape(1, DV), jnp.asarray(_hgrn_level_table()))


def _hgrn_sample_kernel(q_ref, f_ref, v_ref, og_ref, lb_ref, ng_ref, s0_ref, o_ref, snew_ref,
                        qin_ref, kout_ref, eend_ref, oacc_ref, mt_ref, q16_ref, ostage_ref):
    step = pl.program_id(0)
    L, B = DEC_SEQ, DEC_BATCH

    @pl.when(step == 0)
    def _():
        lb = lb_ref[...]
        g, k = _hgrn_gates(f_ref[...], lb)
        qh = _silu(q_ref[...])
        v = v_ref[...]
        bs = []
        acc = None
        for t in range(L):
            gt = g[t * B:(t + 1) * B]
            acc = gt if acc is None else acc + gt
            bs.append(acc)
        b_end = bs[-1]
        eend_ref[...] = jnp.exp(b_end)
        for t in range(L):
            rt = slice(t * B, (t + 1) * B)
            qin_ref[rt, :] = qh[rt] * jnp.exp(bs[t])
            kout_ref[rt, :] = k[rt] * jnp.exp(b_end - bs[t])
            for h in range(HH):
                sl = slice(h * DK, (h + 1) * DK)
                o = jnp.zeros((B, DV), f32)
                for s_ in range(t + 1):
                    rs = slice(s_ * B, (s_ + 1) * B)
                    w = jnp.sum(qh[rt, sl] * k[rs, sl] * jnp.exp(bs[t][:, sl] - bs[s_][:, sl]),
                                axis=-1, keepdims=True)
                    o = o + w * v[rs, sl]
                oacc_ref[rt, sl] = o
        mt_ref[...] = jnp.zeros_like(mt_ref)
        q16_ref[...] = jnp.zeros_like(q16_ref)

    rows = [pl.ds(pl.multiple_of(t * B + step * SBB, SBB), SBB) for t in range(L)]
    q_t = [qin_ref[rows[t], :] for t in range(L)]
    k_t = [kout_ref[rows[t], :] for t in range(L)]
    v_t = [v_ref[rows[t], :] for t in range(L)]
    e_t = eend_ref[pl.ds(pl.multiple_of(step * SBB, SBB), SBB), :]
    for bi in range(SBB):
        for h in range(HH):
            sl = slice(h * DK, (h + 1) * DK)
            s0 = s0_ref[0, bi, h]
            for t in range(L):
                q16_ref[t:t + 1, :] = q_t[t][bi:bi + 1, sl]
            oi = _dot(q16_ref[...].astype(bf16), s0.astype(bf16))
            for t in range(L):
                ostage_ref[t, bi:bi + 1, sl] = oi[t:t + 1]
            mt_ref[0:1, :] = e_t[bi:bi + 1, sl]
            for t in range(L):
                mt_ref[t + 1:t + 2, :] = k_t[t][bi:bi + 1, sl]
            cols = mt_ref[...].T
            sn = s0 * cols[:, 0:1]
            for t in range(L):
                sn = sn + cols[:, t + 1:t + 2] * v_t[t][bi:bi + 1, sl]
            snew_ref[bi, h] = sn
    for t in range(L):
        oacc_ref[rows[t], :] += ostage_ref[t]

    @pl.when(step == pl.num_programs(0) - 1)
    def _():
        gate = _silu(og_ref[...])
        ng = ng_ref[...]
        for h in range(HH):
            sl = slice(h * DK, (h + 1) * DK)
            o_ref[:, sl] = (_rms(oacc_ref[:, sl], ng) * gate[:, sl]).astype(bf16)


def _hgrn_sample(l, proj, lb, onorm_g, state):
    def slab(k):
        return pl.BlockSpec((TS, KW), lambda s: (TP // TS, k))

    return pl.pallas_call(
        _hgrn_sample_kernel,
        grid=(DEC_BATCH // SBB,),
        in_specs=[slab(OFF_HQ // KW), slab(OFF_HF // KW), slab(OFF_HV // KW), slab(OFF_HG // KW),
                  pl.BlockSpec((1, KW), lambda s: (0, 0)),
                  pl.BlockSpec((1, DV), lambda s: (0, 0)),
                  pl.BlockSpec((1, SBB, HH, DK, DV), lambda s: (l, s, 0, 0, 0))],
        out_specs=[pl.BlockSpec((TS, KW), lambda s: (0, 0)),
                   pl.BlockSpec((SBB, HH, DK, DV), lambda s: (s, 0, 0, 0))],
        out_shape=[jax.ShapeDtypeStruct((TS, KW), bf16),
                   jax.ShapeDtypeStruct((DEC_BATCH, HH, DK, DV), f32)],
        scratch_shapes=[pltpu.VMEM((TS, KW), f32), pltpu.VMEM((TS, KW), f32),
                        pltpu.VMEM((DEC_BATCH, KW), f32), pltpu.VMEM((TS, KW), f32),
                        pltpu.VMEM((LANES, DK), f32), pltpu.VMEM((2 * SUBLANES, DK), f32),
                        pltpu.VMEM((DEC_SEQ, SBB, KW), f32)],
        compiler_params=_cparams(("arbitrary",), 56),
        name="hgrn_sample",
    )(proj, proj, proj, proj, lb.reshape(1, KW), onorm_g.reshape(1, DV), state)


def _head_rms(x, g, n):
    return jnp.concatenate([_rms(x[:, i * HD:(i + 1) * HD], g) for i in range(n)], axis=1)


def _sink_softmax_pv(s, sink_col, vb):
    m = jnp.maximum(jnp.max(s, axis=-1, keepdims=True), sink_col)
    p = jnp.exp(s - m)
    den = jnp.sum(p, axis=-1, keepdims=True) + jnp.exp(sink_col - m)
    return _dot(p.astype(bf16), vb) / den


def _attn_prompt_kernel(sink_ref, q_ref, kc_ref, kp_ref, vc_ref, vp_ref, qg_ref, kg_ref, bias_ref,
                        o_ref, kn_ref):
    i = pl.program_id(1)
    kc = _head_rms(kc_ref[...], kg_ref[...], NKV)
    kp = _head_rms(kp_ref[...], kg_ref[...], NKV)
    kn_ref[...] = kc
    kk = jnp.concatenate([kp, kc], axis=0).astype(bf16)
    vv = jnp.concatenate([vp_ref[...], vc_ref[...]], axis=0).astype(bf16)
    q = q_ref[...]
    col = lax.broadcasted_iota(i32, (GQ * WINDOW, 2 * WINDOW), 1)
    valid = (i > 0) | (col >= WINDOW)
    for n in range(NKV):
        qn = jnp.concatenate(
            [_rms(q[:, (n * GQ + g) * HD:(n * GQ + g + 1) * HD], qg_ref[...]) for g in range(GQ)], axis=0)
        s = _dot_nt(qn.astype(bf16), kk[:, n * HD:(n + 1) * HD]) * (HD ** -0.5)
        s = s + bias_ref[n * GQ:(n + 1) * GQ].reshape(GQ * WINDOW, 2 * WINDOW)
        s = jnp.where(valid, s, NEG)
        sink_col = jnp.concatenate(
            [jnp.full((WINDOW, 1), sink_ref[n * GQ + g], f32) for g in range(GQ)], axis=0)
        o = _sink_softmax_pv(s, sink_col, vv[:, n * HD:(n + 1) * HD])
        for g in range(GQ):
            hh = n * GQ + g
            o_ref[:, hh * HD:(hh + 1) * HD] = o[g * WINDOW:(g + 1) * WINDOW].astype(bf16)


def _attn_prompt(proj, sinks, q_norm_g, k_norm_g, bias):
    nb = SEQ // WINDOW
    kblk, vblk = OFF_AK // KVW, OFF_AV // KVW

    def cur(col):
        return lambda b, i: (b * nb + i, col)

    def prev(col):
        return lambda b, i: (b * nb + jnp.maximum(i - 1, 0), col)

    return pl.pallas_call(
        _attn_prompt_kernel,
        grid=(BATCH, nb),
        in_specs=[pl.BlockSpec(memory_space=pltpu.SMEM),
                  pl.BlockSpec((WINDOW, NH * HD), cur(OFF_AQ // (NH * HD))),
                  pl.BlockSpec((WINDOW, KVW), cur(kblk)),
                  pl.BlockSpec((WINDOW, KVW), prev(kblk)),
                  pl.BlockSpec((WINDOW, KVW), cur(vblk)),
                  pl.BlockSpec((WINDOW, KVW), prev(vblk)),
                  pl.BlockSpec((1, HD), lambda b, i: (0, 0)),
                  pl.BlockSpec((1, HD), lambda b, i: (0, 0)),
                  pl.BlockSpec((NH, WINDOW, 2 * WINDOW), lambda b, i: (0, 0, 0))],
        out_specs=[pl.BlockSpec((WINDOW, NH * HD), lambda b, i: (b * nb + i, 0)),
                   pl.BlockSpec((WINDOW, KVW), lambda b, i: (b * nb + i, 0))],
        out_shape=[jax.ShapeDtypeStruct((TP, NH * HD), bf16),
                   jax.ShapeDtypeStruct((TP, KVW), f32)],
        compiler_params=_cparams(("arbitrary", "arbitrary"), 32),
        name="attn_prompt",
    )(sinks, proj, proj, proj, proj, proj, q_norm_g.reshape(1, HD), k_norm_g.reshape(1, HD), bias)


def _attn_sample_kernel(sink_ref, q_ref, k_ref, v_ref, ck_ref, cv_ref, qg_ref, kg_ref, bias_ref,
                        o_ref, kw_ref, vw_ref):
    pad_rows = 2 * WINDOW - WINDOW - SPAD
    for bi in range(SBB):
        kn = _head_rms(k_ref[:, bi, :], kg_ref[...], NKV)
        vn = v_ref[:, bi, :]
        ck = ck_ref[0, bi]
        cv = cv_ref[0, bi]
        kw_ref[bi, 0:WINDOW - DEC_SEQ, :] = ck[DEC_SEQ:, :]
        kw_ref[bi, WINDOW - DEC_SEQ:WINDOW, :] = kn[0:DEC_SEQ]
        vw_ref[bi, 0:WINDOW - DEC_SEQ, :] = cv[DEC_SEQ:, :]
        vw_ref[bi, WINDOW - DEC_SEQ:WINDOW, :] = vn[0:DEC_SEQ]
        zpad = jnp.zeros((pad_rows, KVW), f32)
        kk = jnp.concatenate([ck, kn, zpad], axis=0).astype(bf16)
        vv = jnp.concatenate([cv, vn, zpad], axis=0).astype(bf16)
        q = q_ref[:, bi, :]
        for n in range(NKV):
            qn = jnp.concatenate(
                [_rms(q[:, (n * GQ + g) * HD:(n * GQ + g + 1) * HD], qg_ref[...]) for g in range(GQ)], axis=0)
            s = _dot_nt(qn.astype(bf16), kk[:, n * HD:(n + 1) * HD]) * (HD ** -0.5)
            s = s + bias_ref[n * GQ:(n + 1) * GQ].reshape(GQ * SPAD, 2 * WINDOW)
            sink_col = jnp.concatenate(
                [jnp.full((SPAD, 1), sink_ref[n * GQ + g], f32) for g in range(GQ)], axis=0)
            o = _sink_softmax_pv(s, sink_col, vv[:, n * HD:(n + 1) * HD])
            for g in range(GQ):
                hh = n * GQ + g
                o_ref[:, bi, hh * HD:(hh + 1) * HD] = o[g * SPAD:(g + 1) * SPAD].astype(bf16)


def _attn_sample(l, q_pad, k_pad, v_pad, cache_k, cache_v, sinks, q_norm_g, k_norm_g, bias):
    return pl.pallas_call(
        _attn_sample_kernel,
        grid=(DEC_BATCH // SBB,),
        in_specs=[pl.BlockSpec(memory_space=pltpu.SMEM),
                  pl.BlockSpec((SPAD, SBB, NH * HD), lambda s: (0, s, 0)),
                  pl.BlockSpec((SPAD, SBB, KVW), lambda s: (0, s, 0)),
                  pl.BlockSpec((SPAD, SBB, KVW), lambda s: (0, s, 0)),
                  pl.BlockSpec((1, SBB, WINDOW, KVW), lambda s: (l, s, 0, 0)),
                  pl.BlockSpec((1, SBB, WINDOW, KVW), lambda s: (l, s, 0, 0)),
                  pl.BlockSpec((1, HD), lambda s: (0, 0)),
                  pl.BlockSpec((1, HD), lambda s: (0, 0)),
                  pl.BlockSpec((NH, SPAD, 2 * WINDOW), lambda s: (0, 0, 0))],
        out_specs=[pl.BlockSpec((SPAD, SBB, NH * HD), lambda s: (0, s, 0)),
                   pl.BlockSpec((SBB, WINDOW, KVW), lambda s: (s, 0, 0)),
                   pl.BlockSpec((SBB, WINDOW, KVW), lambda s: (s, 0, 0))],
        out_shape=[jax.ShapeDtypeStruct((SPAD, DEC_BATCH, NH * HD), bf16),
                   jax.ShapeDtypeStruct((DEC_BATCH, WINDOW, KVW), f32),
                   jax.ShapeDtypeStruct((DEC_BATCH, WINDOW, KVW), f32)],
        compiler_params=_cparams(("arbitrary",), 32),
        name="attn_sample",
    )(sinks, q_pad, k_pad, v_pad, cache_k, cache_v, q_norm_g.reshape(1, HD), k_norm_g.reshape(1, HD), bias)


def _merge_kernel(oh_ref, oa_ref, ga_ref, gb_ref, x_ref, g1p_ref, g1s_ref, wa_ref, wb_ref, wo_ref,
                  o_ref, acc_ref):
    i, j = pl.program_id(0), pl.program_id(1)
    merged = (jax.nn.sigmoid(ga_ref[...]) * _dot(oh_ref[...], wa_ref[0])
              + jax.nn.sigmoid(gb_ref[...]) * _dot(oa_ref[...], wb_ref[0]))
    part = _dot(merged.astype(bf16), wo_ref[0])

    @pl.when(j == 0)
    def _():
        acc_ref[...] = part

    @pl.when(j > 0)
    def _():
        acc_ref[...] += part

    @pl.when(j == pl.num_programs(1) - 1)
    def _():
        g1 = _mod_rows(i >= NPT, g1p_ref, g1s_ref, TM)
        o_ref[...] = x_ref[...] + g1 * acc_ref[...]


def _merge(l, o_h, o_a, proj, x, mod3, mod2, wa_bf, wb_bf, wo_bf):
    g1p, g1s = _mod_specs(2, TPB, NPT)
    return pl.pallas_call(
        _merge_kernel,
        grid=(NT, D // TN_MG),
        in_specs=[pl.BlockSpec((TM, KW), lambda i, j: (i, 0)),
                  pl.BlockSpec((TM, NH * HD), lambda i, j: (i, 0)),
                  pl.BlockSpec((TM, TN_MG), lambda i, j: (i, OFF_GA // TN_MG + j)),
                  pl.BlockSpec((TM, TN_MG), lambda i, j: (i, OFF_GB // TN_MG + j)),
                  pl.BlockSpec((TM, D), lambda i, j: (i, 0)),
                  g1p, g1s,
                  pl.BlockSpec((1, KW, TN_MG), lambda i, j: (l, 0, j)),
                  pl.BlockSpec((1, NH * HD, TN_MG), lambda i, j: (l, 0, j)),
                  pl.BlockSpec((1, TN_MG, D), lambda i, j: (l, j, 0))],
        out_specs=pl.BlockSpec((TM, D), lambda i, j: (i, 0)),
        out_shape=jax.ShapeDtypeStruct((T, D), f32),
        scratch_shapes=[pltpu.VMEM((TM, D), f32)],
        compiler_params=_cparams(("arbitrary", "arbitrary"), 48),
        name="merge",
    )(o_h, o_a, proj, proj, x, mod3, mod2, wa_bf, wb_bf, wo_bf)


def _first_index(hit, idx, big, axis):
    return jnp.min(jnp.where(hit, idx, big), axis=axis, keepdims=True)


def _pack_bf16_pairs(h):
    lo = lax.bitcast_convert_type(h[:, :D // 2].astype(bf16).astype(f32), u32)
    hi = lax.bitcast_convert_type(h[:, D // 2:].astype(bf16).astype(f32), u32)
    return (hi & jnp.uint32(0xFFFF0000)) | (lo >> 16)


def _unpack_bf16_pairs(p):
    lo = lax.bitcast_convert_type(p << 16, f32).astype(bf16)
    hi = lax.bitcast_convert_type(p & jnp.uint32(0xFFFF0000), f32).astype(bf16)
    return lo, hi


def _router_kernel(x_ref, g_ref, scp_ref, scs_ref, shp_ref, shs_ref, wr_ref, rb_ref,
                   hp_ref, ids_ref, wts_ref, rank_ref, cnt_ref):
    i = pl.program_id(0)
    y = _rms(x_ref[...], g_ref[...])
    sc = _mod_rows(i >= NPT, scp_ref, scs_ref, TM)
    sh = _mod_rows(i >= NPT, shp_ref, shs_ref, TM)
    h = y * (1.0 + sc) + sh
    hp_ref[...] = _pack_bf16_pairs(h)
    wr = wr_ref[0]
    w1 = wr.astype(bf16)
    w2 = (wr - w1.astype(f32)).astype(bf16)
    h1 = h.astype(bf16)
    h2 = (h - h1.astype(f32)).astype(bf16)
    logits = _dot_nt(w1, h1) + (_dot_nt(w1, h2) + _dot_nt(w2, h1))
    scores = jax.nn.sigmoid(logits)
    biased = scores + rb_ref[...]
    b3 = biased.reshape(NG, EPG, TM)
    e_in = lax.broadcasted_iota(i32, (NG, EPG, TM), 1).astype(f32)
    m1 = jnp.max(b3, axis=1, keepdims=True)
    first = _first_index(b3 == m1, e_in, float(EPG), 1)
    m2 = jnp.max(jnp.where(e_in == first, -jnp.inf, b3), axis=1, keepdims=True)
    gs = (m1 + m2).reshape(NG, TM)
    g_idx = lax.broadcasted_iota(i32, (NG, TM), 0).astype(f32)
    gsel = jnp.zeros((NG, TM), f32)
    for _ in range(TOPKG):
        m = jnp.max(gs, axis=0, keepdims=True)
        pick = g_idx == _first_index(gs == m, g_idx, float(NG), 0)
        gsel = jnp.where(pick, 1.0, gsel)
        gs = jnp.where(pick, -jnp.inf, gs)
    emask = jnp.broadcast_to(gsel.reshape(NG, 1, TM), (NG, EPG, TM)).reshape(NE, TM)
    cur = jnp.where(emask > 0.5, biased, NEG)
    e_idx = lax.broadcasted_iota(i32, (NE, TM), 0).astype(f32)
    ids, ws, picks = [], [], []
    for _ in range(TOPK):
        m = jnp.max(cur, axis=0, keepdims=True)
        first = _first_index(cur == m, e_idx, float(NE), 0)
        pick = e_idx == first
        ids.append(first)
        picks.append(pick)
        ws.append(jnp.sum(jnp.where(pick, scores, 0.0), axis=0, keepdims=True))
        cur = jnp.where(pick, -jnp.inf, cur)
    w = jnp.concatenate(ws, axis=0)
    ids_ref[...] = jnp.concatenate(ids, axis=0).astype(i32)
    wts_ref[...] = w / jnp.sum(w, axis=0, keepdims=True) * ROUTED_SCALE

    @pl.when(i == 0)
    def _():
        cnt_ref[...] = jnp.zeros_like(cnt_ref)

    onehot = jnp.zeros((NE, TM), f32)
    for pick in picks:
        onehot = jnp.where(pick, 1.0, onehot)
    before = (lax.broadcasted_iota(i32, (TM, TM), 0) < lax.broadcasted_iota(i32, (TM, TM), 1)).astype(bf16)
    base = cnt_ref[:, 0:1] + _dot(onehot.astype(bf16), before)
    rank_ref[...] = jnp.concatenate(
        [jnp.sum(jnp.where(pick, base, 0.0), axis=0, keepdims=True) for pick in picks], axis=0).astype(i32)
    cnt_ref[...] += jnp.sum(onehot, axis=1, keepdims=True)


def _router(l, x1, norm_g, mod3, mod2, w_router_t, router_bias):
    scp, scs = _mod_specs(4, TPB, NPT)
    shp, shs = _mod_specs(3, TPB, NPT)
    return pl.pallas_call(
        _router_kernel,
        grid=(NT,),
        in_specs=[pl.BlockSpec((TM, D), lambda i: (i, 0)),
                  pl.BlockSpec((1, D), lambda i: (0, 0)),
                  scp, scs, shp, shs,
                  pl.BlockSpec((1, NE, D), lambda i: (l, 0, 0)),
                  pl.BlockSpec((NE, 1), lambda i: (0, 0))],
        out_specs=[pl.BlockSpec((TM, D // 2), lambda i: (i, 0)),
                   pl.BlockSpec((TOPK, TM), lambda i: (0, i)),
                   pl.BlockSpec((TOPK, TM), lambda i: (0, i)),
                   pl.BlockSpec((TOPK, TM), lambda i: (0, i)),
                   pl.BlockSpec((NE, LANES), lambda i: (0, 0))],
        out_shape=[jax.ShapeDtypeStruct((T, D // 2), u32),
                   jax.ShapeDtypeStruct((TOPK, T), i32),
                   jax.ShapeDtypeStruct((TOPK, T), f32),
                   jax.ShapeDtypeStruct((TOPK, T), i32),
                   jax.ShapeDtypeStruct((NE, LANES), f32)],
        compiler_params=_cparams(("arbitrary",), 48),
        name="router",
    )(x1, norm_g.reshape(1, D), mod3, mod2, mod3, mod2, w_router_t, router_bias.reshape(NE, 1))


def _block_tables(cnt):
    counts = cnt[:, 0].astype(i32)
    padded = (counts + BLK - 1) // BLK * BLK
    pend = jnp.cumsum(padded)
    pstart = pend - padded
    blk_start = jnp.arange(NBLK, dtype=i32) * BLK
    block_e = jnp.minimum(jnp.sum(blk_start[:, None] >= pend[None, :], axis=1), NE - 1).astype(i32)
    n_used = (pend[-1] // BLK).astype(i32).reshape(1)
    return block_e, n_used, pstart.astype(i32), (pstart + counts).astype(i32), (padded - counts).astype(i32)


def _pos_kernel(ps_ref, ids_ref, rank_ref, pos_ref):
    ids = ids_ref[...]
    acc = rank_ref[...]
    for e in range(NE):
        acc = acc + jnp.where(ids == e, ps_ref[e], 0)
    pos_ref[...] = acc


def _slot_positions(pstart, ids_t, rank_t):
    return pl.pallas_call(
        _pos_kernel,
        grid=(1,),
        in_specs=[pl.BlockSpec(memory_space=pltpu.SMEM),
                  pl.BlockSpec((TOPK, T), lambda i: (0, 0)),
                  pl.BlockSpec((TOPK, T), lambda i: (0, 0))],
        out_specs=pl.BlockSpec((TOPK, T), lambda i: (0, 0)),
        out_shape=jax.ShapeDtypeStruct((TOPK, T), i32),
        compiler_params=_cparams(("arbitrary",), 16),
        name="slot_positions",
    )(pstart, ids_t, rank_t)


def _row(ref3, r):
    return ref3.at[r >> 3, pl.ds(r & 7, 1), :]


def _dispatch_kernel(pad0_ref, padn_ref, nu_ref, pos_ref, hp_hbm, xs_hbm, zero_ref, sem):
    i = pl.program_id(0)
    tiles = BLK // SUBLANES

    def tok_copy(k, j, u):
        return pltpu.make_async_copy(hp_hbm.at[i * (TMF // SUBLANES) + j, pl.ds(u, 1), :],
                                     _row(xs_hbm, pos_ref[0, 0, k * TMF + j * SUBLANES + u]), sem)

    def pad_copy(e, r):
        return pltpu.make_async_copy(zero_ref.at[0, pl.ds(0, 1), :], _row(xs_hbm, pad0_ref[e] + r), sem)

    def tail_copy(b):
        return pltpu.make_async_copy(zero_ref, xs_hbm.at[pl.ds(b * tiles, tiles)], sem)

    def for_tokens(fn):
        for k in range(TOPK):
            def body(j, c):
                for u in range(SUBLANES):
                    fn(tok_copy(k, j, u), u)
                return c
            lax.fori_loop(0, TMF // SUBLANES, body, 0)

    def for_pads(fn):
        def per_expert(e, c):
            def body(r, c2):
                fn(pad_copy(e, r), 0)
                return c2
            return lax.fori_loop(0, padn_ref[e], body, c)
        lax.fori_loop(0, NE, per_expert, 0)

        def per_block(b, c):
            fn(tail_copy(b), 0)
            return c
        lax.fori_loop(nu_ref[0], NBLK, per_block, 0)

    @pl.when(i == 0)
    def _():
        zero_ref[...] = jnp.zeros_like(zero_ref)
        for_pads(lambda cp, u: cp.start(priority=u % 2))

    for_tokens(lambda cp, u: cp.start(priority=u % 2))
    for_tokens(lambda cp, u: cp.wait())

    @pl.when(i == 0)
    def _():
        for_pads(lambda cp, u: cp.wait())


def _dispatch(pad_start, pad_cnt, n_used, pos_tiles, hp):
    grid_spec = pltpu.PrefetchScalarGridSpec(
        num_scalar_prefetch=3,
        grid=(T // TMF,),
        in_specs=[pl.BlockSpec((1, 1, TOPK * TMF), lambda i, p0, pn, nu: (i, 0, 0), memory_space=pltpu.SMEM),
                  pl.BlockSpec(memory_space=pl.ANY)],
        out_specs=pl.BlockSpec(memory_space=pl.ANY),
        scratch_shapes=[pltpu.VMEM((BLK // SUBLANES, SUBLANES, D // 2), u32), pltpu.SemaphoreType.DMA(())],
    )
    xs = pl.pallas_call(
        _dispatch_kernel,
        grid_spec=grid_spec,
        out_shape=jax.ShapeDtypeStruct((NROW // SUBLANES, SUBLANES, D // 2), u32),
        compiler_params=_cparams(("arbitrary",), 16),
        name="dispatch",
    )(pad_start, pad_cnt, n_used, pos_tiles, hp.reshape(T // SUBLANES, SUBLANES, D // 2))
    return xs.reshape(NROW, D // 2)


def _expert_kernel(be_ref, nu_ref, x_ref, wg_ref, wu_ref, wd_ref, y_ref, wgb, wub, wdb):
    i = pl.program_id(0)

    @pl.when(i >= nu_ref[0])
    def _():
        y_ref[...] = jnp.zeros_like(y_ref)

    @pl.when(i < nu_ref[0])
    def _():
        @pl.when((i == 0) | (be_ref[i] != be_ref[jnp.maximum(i - 1, 0)]))
        def _():
            wgb[...] = wg_ref[0, 0].astype(bf16)
            wub[...] = wu_ref[0, 0].astype(bf16)
            wdb[...] = wd_ref[0, 0].astype(bf16)

        lo, hi = _unpack_bf16_pairs(x_ref[...])
        half = D // 2
        g = _dot(lo, wgb[0:half, :]) + _dot(hi, wgb[half:D, :])
        u = _dot(lo, wub[0:half, :]) + _dot(hi, wub[half:D, :])
        y_ref[...] = _dot((_silu(g) * u).astype(bf16), wdb[...])


def _experts(l, xs, block_e, n_used, w_gate, w_up, w_down):
    def blk(i, be, nu):
        return jnp.minimum(i, nu[0] - 1)

    grid_spec = pltpu.PrefetchScalarGridSpec(
        num_scalar_prefetch=2,
        grid=(NBLK,),
        in_specs=[pl.BlockSpec((BLK, D // 2), lambda i, be, nu: (blk(i, be, nu), 0)),
                  pl.BlockSpec((1, 1, D, DE), lambda i, be, nu: (l, be[blk(i, be, nu)], 0, 0)),
                  pl.BlockSpec((1, 1, D, DE), lambda i, be, nu: (l, be[blk(i, be, nu)], 0, 0)),
                  pl.BlockSpec((1, 1, DE, D), lambda i, be, nu: (l, be[blk(i, be, nu)], 0, 0))],
        out_specs=pl.BlockSpec((BLK, D), lambda i, be, nu: (i, 0)),
        scratch_shapes=[pltpu.VMEM((D, DE), bf16), pltpu.VMEM((D, DE), bf16), pltpu.VMEM((DE, D), bf16)],
    )
    return pl.pallas_call(
        _expert_kernel,
        grid_spec=grid_spec,
        out_shape=jax.ShapeDtypeStruct((NROW, D), f32),
        compiler_params=_cparams(("arbitrary",), 56),
        name="experts",
    )(block_e, n_used, xs, w_gate, w_up, w_down)


def _final_kernel(pos_ref, posn_ref, x_ref, hp_ref, w_ref, g2p_ref, g2s_ref, sg_ref, su_ref, sd_ref,
                  y_hbm, o_ref, ybuf, sem):
    i = pl.program_id(0)
    n = pl.num_programs(0)
    slot = i % 2

    def row_copy(pref, k, j, u, s):
        return pltpu.make_async_copy(_row(y_hbm, pref[0, 0, k * TMF + j * SUBLANES + u]),
                                     ybuf.at[s, k, j, pl.ds(u, 1), :], sem.at[s])

    def for_rows(pref, s, fn):
        for k in range(TOPK):
            def body(j, c):
                for u in range(SUBLANES):
                    fn(row_copy(pref, k, j, u, s), u)
                return c
            lax.fori_loop(0, TMF // SUBLANES, body, 0)

    @pl.when(i == 0)
    def _():
        for_rows(pos_ref, 0, lambda cp, u: cp.start(priority=u % 2))

    @pl.when(i + 1 < n)
    def _():
        for_rows(posn_ref, 1 - slot, lambda cp, u: cp.start(priority=u % 2))

    lo, hi = _unpack_bf16_pairs(hp_ref[...])
    hb = jnp.concatenate([lo, hi], axis=1)
    shared = _dot((_silu(_dot(hb, sg_ref[0])) * _dot(hb, su_ref[0])).astype(bf16), sd_ref[0])

    for_rows(pos_ref, slot, lambda cp, u: cp.wait())

    w = w_ref[...]
    routed = w[:, 0:1] * ybuf[slot, 0].reshape(TMF, D)
    for k in range(1, TOPK):
        routed = routed + w[:, k:k + 1] * ybuf[slot, k].reshape(TMF, D)
    g2 = _mod_rows(i >= TP // TMF, g2p_ref, g2s_ref, TMF)
    o_ref[...] = x_ref[...] + g2 * (shared + routed)


def _final(l, x1, hp, wts, pos_tiles, mod3, mod2, sg_bf, su_bf, sd_bf, yb):
    g2p, g2s = _mod_specs(5, SEQ // TMF, TP // TMF)
    ntile = T // TMF
    return pl.pallas_call(
        _final_kernel,
        grid=(ntile,),
        in_specs=[pl.BlockSpec((1, 1, TOPK * TMF), lambda i: (i, 0, 0), memory_space=pltpu.SMEM),
                  pl.BlockSpec((1, 1, TOPK * TMF), lambda i: (jnp.minimum(i + 1, ntile - 1), 0, 0),
                               memory_space=pltpu.SMEM),
                  pl.BlockSpec((TMF, D), lambda i: (i, 0)),
                  pl.BlockSpec((TMF, D // 2), lambda i: (i, 0)),
                  pl.BlockSpec((TMF, TOPK), lambda i: (i, 0)),
                  g2p, g2s,
                  pl.BlockSpec((1, D, DE), lambda i: (l, 0, 0)),
                  pl.BlockSpec((1, D, DE), lambda i: (l, 0, 0)),
                  pl.BlockSpec((1, DE, D), lambda i: (l, 0, 0)),
                  pl.BlockSpec(memory_space=pl.ANY)],
        out_specs=pl.BlockSpec((TMF, D), lambda i: (i, 0)),
        out_shape=jax.ShapeDtypeStruct((T, D), f32),
        scratch_shapes=[pltpu.VMEM((2, TOPK, TMF // SUBLANES, SUBLANES, D), f32),
                        pltpu.SemaphoreType.DMA((2,))],
        compiler_params=_cparams(("arbitrary",), 56),
        name="moe_combine",
    )(pos_tiles, pos_tiles, x1, hp, wts, mod3, mod2, sg_bf, su_bf, sd_bf,
      yb.reshape(NROW // SUBLANES, SUBLANES, D))


def _layer(l, x, mod, lb, state, cache_k, cache_v, bias_p, bias_s, norm1_g, norm2_g, w_in_bf, onorm_g,
           q_norm_g, k_norm_g, sinks, wa_bf, wb_bf, wo_bf, w_router_t, router_bias,
           w_eg, w_eu, w_ed, sg_bf, su_bf, sd_bf):
    mod3 = mod.reshape(mod.shape[0], 1, 6 * D)
    proj = _in_proj(l, x, norm1_g, mod3, mod, w_in_bf)

    oh_p, s_p = _hgrn_prompt(proj, lb, onorm_g)
    oh_s, s_s = _hgrn_sample(l, proj, lb, onorm_g, state)

    oa_p, kn_p = _attn_prompt(proj, sinks, q_norm_g, k_norm_g, bias_p)
    samp = proj[TP:].reshape(DEC_SEQ, DEC_BATCH, IN_WIDTH)
    padw = ((0, SPAD - DEC_SEQ), (0, 0), (0, 0))
    q_pad = jnp.pad(samp[:, :, OFF_AQ:OFF_AK], padw)
    k_pad = jnp.pad(samp[:, :, OFF_AK:OFF_AV], padw)
    v_pad = jnp.pad(samp[:, :, OFF_AV:OFF_GA], padw)
    oa_s, kw_s, vw_s = _attn_sample(l, q_pad, k_pad, v_pad, cache_k, cache_v, sinks, q_norm_g, k_norm_g, bias_s)

    o_h = jnp.concatenate([oh_p, oh_s], axis=0)
    o_a = jnp.concatenate([oa_p, oa_s[:DEC_SEQ].reshape(TS, NH * HD)], axis=0)
    x1 = _merge(l, o_h, o_a, proj, x, mod3, mod, wa_bf, wb_bf, wo_bf)

    hp, ids_t, wts_t, rank_t, cnt = _router(l, x1, norm2_g, mod3, mod, w_router_t, router_bias)
    block_e, n_used, pstart, pad_start, pad_cnt = _block_tables(cnt)
    pos = _slot_positions(pstart, ids_t, rank_t)
    pos_tiles = pos.reshape(TOPK, T // TMF, TMF).transpose(1, 0, 2).reshape(T // TMF, 1, TOPK * TMF)
    xs = _dispatch(pad_start, pad_cnt, n_used, pos_tiles, hp)
    yb = _experts(l, xs, block_e, n_used, w_eg, w_eu, w_ed)
    x2 = _final(l, x1, hp, wts_t.T, pos_tiles, mod3, mod, sg_bf, su_bf, sd_bf, yb)

    k_p = kn_p.reshape(BATCH, SEQ, NKV, HD)[:, SEQ - WINDOW:]
    v_p = proj[:TP, OFF_AV:OFF_GA].reshape(BATCH, SEQ, NKV, HD)[:, SEQ - WINDOW:]
    return (x2, s_p, s_s, k_p, v_p, kw_s.reshape(DEC_BATCH, WINDOW, NKV, HD),
            vw_s.reshape(DEC_BATCH, WINDOW, NKV, HD))


def kernel(x_prompt, x_sample, state_hgrn, cache_k_win, cache_v_win, c_prompt, c_sample, norm1_g, norm2_g,
           w_ada, b_ada, w_in, hgrn_lb_logits, hgrn_onorm_g, q_norm_g, k_norm_g, attn_sinks, rel_bias,
           w_branch_a, w_branch_b, w_out, w_router, router_bias, w_exp_gate, w_exp_up, w_exp_down,
           w_sh_gate, w_sh_up, w_sh_down):
    p_lb = jax.nn.softmax(hgrn_lb_logits.astype(f32), axis=0)
    lower_bounds = jnp.clip(jnp.cumsum(p_lb, axis=0) - p_lb[0], 0.0, 1.0 - 1e-6)

    c_all = jnp.concatenate([c_sample, c_prompt, jnp.zeros((SUBLANES - BATCH, D), f32)], axis=0)
    mod = _ada_mod(c_all, w_ada, b_ada)
    bias_p = _bias_table(rel_bias, _prompt_buckets())
    bias_s = _bias_table(rel_bias, _sample_buckets())

    x = jnp.concatenate([x_prompt.reshape(TP, D), x_sample.transpose(1, 0, 2).reshape(TS, D)], axis=0)
    cache_k = cache_k_win.reshape(DEPTH, DEC_BATCH, WINDOW, KVW)
    cache_v = cache_v_win.reshape(DEPTH, DEC_BATCH, WINDOW, KVW)
    w_in_bf, wa_bf, wb_bf, wo_bf = (w.astype(bf16) for w in (w_in, w_branch_a, w_branch_b, w_out))
    sg_bf, su_bf, sd_bf = (w.astype(bf16) for w in (w_sh_gate, w_sh_up, w_sh_down))
    w_router_t = w_router.transpose(0, 2, 1)
    sp_l, ss_l, kp_l, vp_l, ks_l, vs_l = [], [], [], [], [], []
    for l in range(DEPTH):
        x, s_p, s_s, k_p, v_p, k_s, v_s = _layer(
            l, x, mod[l], lower_bounds[l], state_hgrn, cache_k, cache_v, bias_p, bias_s,
            norm1_g[l], norm2_g[l], w_in_bf, hgrn_onorm_g[l], q_norm_g[l], k_norm_g[l], attn_sinks[l],
            wa_bf, wb_bf, wo_bf, w_router_t, router_bias[l],
            w_exp_gate, w_exp_up, w_exp_down, sg_bf, su_bf, sd_bf)
        sp_l.append(s_p); ss_l.append(s_s); kp_l.append(k_p); vp_l.append(v_p); ks_l.append(k_s); vs_l.append(v_s)

    y_prompt = x[:TP].reshape(BATCH, SEQ, D)
    y_sample = x[TP:].reshape(DEC_SEQ, DEC_BATCH, D).transpose(1, 0, 2)
    return (y_prompt, y_sample, jnp.stack(sp_l), jnp.stack(ss_l), jnp.stack(kp_l), jnp.stack(vp_l),
            jnp.stack(ks_l), jnp.stack(vs_l))
```

```python
import functools
import math

import numpy as np
import jax
import jax.numpy as jnp
from jax import lax
from jax.experimental import pallas as pl
from jax.experimental.pallas import tpu as pltpu

f32 = jnp.float32
bf16 = jnp.bfloat16
i32 = jnp.int32
u32 = jnp.uint32

D = 2048
BATCH, SEQ = 2, 4096
DEPTH = 2
DEC_BATCH, DEC_SEQ = 128, 4
HH, DK, DV = 8, 128, 128
KW = HH * DK
NH, NKV, HD = 16, 4, 64
GQ = NH // NKV
KVW = NKV * HD
WINDOW = 128
NUM_BUCKETS, MAX_DISTANCE = 32, 128
NE, NG, TOPKG, TOPK, DE = 64, 8, 4, 8, 512
EPG = NE // NG
ROUTED_SCALE = 2.5
IN_WIDTH = 2 * KW + 2 * KW + NH * HD + 2 * KVW + 2 * D
EPS = 1e-6
NEG = -1e30
TINY = 1e-30

OFF_HQ, OFF_HF, OFF_HV, OFF_HG = 0, KW, 2 * KW, 3 * KW
OFF_AQ = 4 * KW
OFF_AK = OFF_AQ + NH * HD
OFF_AV = OFF_AK + KVW
OFF_GA = OFF_AV + KVW
OFF_GB = OFF_GA + D

TP = BATCH * SEQ
TS = DEC_BATCH * DEC_SEQ
T = TP + TS
TM = 512
NPT = TP // TM
NT = T // TM
TPB = SEQ // TM
TN_IN = 2432
TN_MG = 512
TMF = 128
HC = 256
HGRN_WIDTHS = (128, 64, 32, 16, 8)
HGRN_DIAG = 8
EXP_CLAMP = 80.0
SBB = 8
SPAD = 8
BLK = 256
NA = T * TOPK
NBLK = NA // BLK + NE
NROW = NBLK * BLK

V7X_VMEM_BYTES = 64 * 1024 * 1024
SUBLANES, LANES = 8, 128
ROW_TILES_X = D // 2 // LANES
ROW_TILES_Y = D // LANES


def _cparams(sem, vmem_mib):
    assert vmem_mib * 1024 * 1024 < V7X_VMEM_BYTES
    return pltpu.CompilerParams(dimension_semantics=sem, vmem_limit_bytes=vmem_mib * 1024 * 1024)


def _silu(x):
    return x * jax.nn.sigmoid(x)


def _rms(x, g):
    return x * lax.rsqrt(jnp.mean(x * x, axis=-1, keepdims=True) + EPS) * g


def _dot(a, b):
    return jnp.dot(a, b, preferred_element_type=f32)


def _dot_nt(a, b):
    return lax.dot_general(a, b, (((1,), (1,)), ((), ())), preferred_element_type=f32)


def _dot_tn(a, b):
    return lax.dot_general(a, b, (((0,), (0,)), ((), ())), preferred_element_type=f32)


def _split3(x):
    a = x.astype(bf16)
    r = x - a.astype(f32)
    b = r.astype(bf16)
    c = (r - b.astype(f32)).astype(bf16)
    return a, b, c


def _mod_rows(is_sample, p_ref, s_ref, rows):
    p = p_ref[0]
    s = s_ref[...]
    if rows != DEC_BATCH:
        s = jnp.broadcast_to(s[None], (rows // DEC_BATCH, DEC_BATCH, D)).reshape(rows, D)
    return jnp.where(is_sample, s, p)


def _mod_specs(chunk, tiles_per_seq, n_prompt_tiles, extra_args=0):
    def pmap(i, *_):
        return (DEC_BATCH + jnp.minimum(i // tiles_per_seq, BATCH - 1), 0, chunk)

    def smap(i, *_):
        return (0, chunk)

    return (pl.BlockSpec((1, 1, D), pmap), pl.BlockSpec((DEC_BATCH, D), smap))


def _ada_kernel(c_ref, w_ref, b_ref, o_ref):
    c = c_ref[...]
    a = _silu(c).astype(bf16)
    o_ref[0] = _dot(a, w_ref[0].astype(bf16)) + b_ref[0]


def _ada_mod(c_all, w_ada, b_ada):
    rows = c_all.shape[0]
    tn = 1024
    return pl.pallas_call(
        _ada_kernel,
        grid=(DEPTH, 6 * D // tn),
        in_specs=[pl.BlockSpec((rows, D), lambda l, j: (0, 0)),
                  pl.BlockSpec((1, D, tn), lambda l, j: (l, 0, j)),
                  pl.BlockSpec((1, 1, tn), lambda l, j: (l, 0, j))],
        out_specs=pl.BlockSpec((1, rows, tn), lambda l, j: (l, 0, j)),
        out_shape=jax.ShapeDtypeStruct((DEPTH, rows, 6 * D), f32),
        compiler_params=_cparams(("arbitrary", "arbitrary"), 40),
        name="ada_mod",
    )(c_all, w_ada, b_ada.reshape(DEPTH, 1, 6 * D))


def _bucket_np(dist):
    n = np.maximum(dist, 0)
    exact = NUM_BUCKETS // 2
    nf = np.maximum(n, 1).astype(np.float32)
    large = exact + (np.log(nf / np.float32(exact)) / np.float32(math.log(MAX_DISTANCE / exact))
                     * np.float32(NUM_BUCKETS - exact)).astype(np.int32)
    return np.where(n < exact, n, np.clip(large, exact, NUM_BUCKETS - 1)).astype(np.int32)


def _prompt_buckets():
    j = np.arange(2 * WINDOW)
    dist = np.arange(WINDOW)[:, None] + WINDOW - j[None, :]
    band = (dist >= 0) & (dist < WINDOW)
    return np.where(band, _bucket_np(dist), -1).astype(np.int32)


def _sample_buckets():
    l = np.arange(SPAD)[:, None]
    j = np.arange(2 * WINDOW)[None, :]
    dist = WINDOW + l - j
    ok = (dist >= 0) & (dist < WINDOW) & (l < DEC_SEQ) & (j < WINDOW + DEC_SEQ)
    return np.where(ok, _bucket_np(dist), -1).astype(np.int32)


def _bias_kernel(tab_ref, bk_ref, o_ref):
    h = pl.program_id(0)
    bk = bk_ref[...]
    acc = jnp.zeros(bk.shape, f32)
    for i in range(NUM_BUCKETS):
        acc = jnp.where(bk == i, tab_ref[i, h], acc)
    o_ref[0] = jnp.where(bk < 0, NEG, acc)


def _bias_table(rel_bias, buckets):
    r, c = buckets.shape
    return pl.pallas_call(
        _bias_kernel,
        grid=(NH,),
        in_specs=[pl.BlockSpec(memory_space=pltpu.SMEM),
                  pl.BlockSpec((r, c), lambda h: (0, 0))],
        out_specs=pl.BlockSpec((1, r, c), lambda h: (h, 0, 0)),
        out_shape=jax.ShapeDtypeStruct((NH, r, c), f32),
        compiler_params=_cparams(("arbitrary",), 16),
        name="bias_table",
    )(rel_bias, jnp.asarray(buckets))


def _inproj_kernel(x_ref, g_ref, scp_ref, scs_ref, shp_ref, shs_ref, w_ref, o_ref, h_ref):
    i = pl.program_id(0)

    @pl.when(pl.program_id(1) == 0)
    def _():
        y = _rms(x_ref[...], g_ref[...])
        sc = _mod_rows(i >= NPT, scp_ref, scs_ref, TM)
        sh = _mod_rows(i >= NPT, shp_ref, shs_ref, TM)
        h_ref[...] = (y * (1.0 + sc) + sh).astype(bf16)

    o_ref[...] = _dot(h_ref[...], w_ref[0])


def _in_proj(l, x, norm_g, mod3, mod2, w_in_bf):
    scp, scs = _mod_specs(1, TPB, NPT)
    shp, shs = _mod_specs(0, TPB, NPT)
    return pl.pallas_call(
        _inproj_kernel,
        grid=(NT, IN_WIDTH // TN_IN),
        in_specs=[pl.BlockSpec((TM, D), lambda i, j: (i, 0)),
                  pl.BlockSpec((1, D), lambda i, j: (0, 0)),
                  scp, scs, shp, shs,
                  pl.BlockSpec((1, D, TN_IN), lambda i, j: (l, 0, j))],
        out_specs=pl.BlockSpec((TM, TN_IN), lambda i, j: (i, j)),
        out_shape=jax.ShapeDtypeStruct((T, IN_WIDTH), f32),
        scratch_shapes=[pltpu.VMEM((TM, D), bf16)],
        compiler_params=_cparams(("arbitrary", "arbitrary"), 56),
        name="in_proj",
    )(x, norm_g.reshape(1, D), mod3, mod2, mod3, mod2, w_in_bf)


def _hgrn_gates(z, lb):
    f = lb + (1.0 - lb) * jax.nn.sigmoid(z)
    g = jnp.log(jnp.maximum(f, TINY))
    k = (1.0 - lb) * jax.nn.sigmoid(-z)
    return g, k


def _bcast_block_row(b, period, row):
    c, w = b.shape
    b3 = b.reshape(c // period, period, w)
    return jnp.broadcast_to(b3[:, row:row + 1, :], b3.shape).reshape(c, w)


def _hgrn_level_table():
    t = np.arange(HC)[:, None]
    s = np.arange(HC)[None, :]
    lev = np.full((HC, HC), -1, np.int32)
    for li, w in enumerate(HGRN_WIDTHS):
        m = (t // (2 * w) == s // (2 * w)) & ((t // w) % 2 == 1) & ((s // w) % 2 == 0)
        lev[m] = li
    lev[(t // HGRN_DIAG == s // HGRN_DIAG) & (s <= t)] = len(HGRN_WIDTHS)
    return lev


def _hgrn_prompt_kernel(q_ref, f_ref, v_ref, og_ref, lb_ref, ng_ref, lev_ref, o_ref, sfin_ref, st_ref):
    c = pl.program_id(1)

    @pl.when(c == 0)
    def _():
        st_ref[...] = jnp.zeros_like(st_ref)

    lb = lb_ref[...]
    g, k = _hgrn_gates(f_ref[...], lb)
    qh = _silu(q_ref[...])
    vb = v_ref[...].astype(bf16)
    r = lax.broadcasted_iota(i32, (HC, HC), 0)
    s = lax.broadcasted_iota(i32, (HC, HC), 1)
    tri = (r >= s).astype(bf16)
    g1, g2, g3 = _split3(g)
    b = _dot(tri, g1) + _dot(tri, g2) + _dot(tri, g3)
    lev = lev_ref[...]

    qs, ks = [], []
    for w in HGRN_WIDTHS:
        m = _bcast_block_row(b, 2 * w, w - 1)
        qs.append((qh * jnp.exp(jnp.minimum(b - m, 0.0))).astype(bf16))
        ks.append((k * jnp.exp(jnp.minimum(m - b, 0.0))).astype(bf16))
    m = _bcast_block_row(b, HGRN_DIAG, 0)
    qs.append((qh * jnp.exp(jnp.minimum(b - m, 0.0))).astype(bf16))
    ks.append((k * jnp.exp(jnp.minimum(m - b, EXP_CLAMP))).astype(bf16))
    b_end = b[HC - 1:HC, :]
    q_in = (qh * jnp.exp(b)).astype(bf16)
    k_out = (k * jnp.exp(b_end - b)).astype(bf16)
    e_end = jnp.exp(b_end)
    gate = _silu(og_ref[...])
    ng = ng_ref[...]

    for h in range(HH):
        sl = slice(h * DK, (h + 1) * DK)
        a = jnp.zeros((HC, HC), f32)
        for li in range(len(qs)):
            a = jnp.where(lev == li, _dot_nt(qs[li][:, sl], ks[li][:, sl]), a)
        st = st_ref[h]
        o = _dot(a.astype(bf16), vb[:, sl]) + _dot_nt(q_in[:, sl], st.astype(bf16))
        st_new = st * e_end[:, sl] + _dot_tn(vb[:, sl], k_out[:, sl])
        st_ref[h] = st_new
        o_ref[:, sl] = (_rms(o, ng) * gate[:, sl]).astype(bf16)

        @pl.when(c == pl.num_programs(1) - 1)
        def _():
            sfin_ref[0, h] = st_new.T


def _hgrn_prompt(proj, lb, onorm_g):
    nc = SEQ // HC

    def slab(k):
        return pl.BlockSpec((HC, KW), lambda b, c: (b * nc + c, k))

    return pl.pallas_call(
        _hgrn_prompt_kernel,
        grid=(BATCH, nc),
        in_specs=[slab(OFF_HQ // KW), slab(OFF_HF // KW), slab(OFF_HV // KW), slab(OFF_HG // KW),
                  pl.BlockSpec((1, KW), lambda b, c: (0, 0)),
                  pl.BlockSpec((1, DV), lambda b, c: (0, 0)),
                  pl.BlockSpec((HC, HC), lambda b, c: (0, 0))],
        out_specs=[pl.BlockSpec((HC, KW), lambda b, c: (b * nc + c, 0)),
                   pl.BlockSpec((1, HH, DK, DV), lambda b, c: (b, 0, 0, 0))],
        out_shape=[jax.ShapeDtypeStruct((TP, KW), bf16),
                   jax.ShapeDtypeStruct((BATCH, HH, DK, DV), f32)],
        scratch_shapes=[pltpu.VMEM((HH, DV, DK), f32)],
        compiler_params=_cparams(("arbitrary", "arbitrary"), 48),
        name="hgrn_prompt",
    )(proj, proj, proj, proj, lb.reshape(1, KW), onorm_g.reshape(1, DV), jnp.asarray(_hgrn_level_table()))


def _hgrn_sample_kernel(q_ref, f_ref, v_ref, og_ref, lb_ref, ng_ref, s0_ref, o_ref, snew_ref,
                        qin_ref, kout_ref, eend_ref, oacc_ref, mt_ref, q16_ref, ostage_ref):
    step = pl.program_id(0)
    L, B = DEC_SEQ, DEC_BATCH

    @pl.when(step == 0)
    def _():
        lb = lb_ref[...]
        g, k = _hgrn_gates(f_ref[...], lb)
        qh = _silu(q_ref[...])
        v = v_ref[...]
        bs = []
        acc = None
        for t in range(L):
            gt = g[t * B:(t + 1) * B]
            acc = gt if acc is None else acc + gt
            bs.append(acc)
        b_end = bs[-1]
        eend_ref[...] = jnp.exp(b_end)
        for t in range(L):
            rt = slice(t * B, (t + 1) * B)
            qin_ref[rt, :] = qh[rt] * jnp.exp(bs[t])
            kout_ref[rt, :] = k[rt] * jnp.exp(b_end - bs[t])
            for h in range(HH):
                sl = slice(h * DK, (h + 1) * DK)
                o = jnp.zeros((B, DV), f32)
                for s_ in range(t + 1):
                    rs = slice(s_ * B, (s_ + 1) * B)
                    w = jnp.sum(qh[rt, sl] * k[rs, sl] * jnp.exp(bs[t][:, sl] - bs[s_][:, sl]),
                                axis=-1, keepdims=True)
                    o = o + w * v[rs, sl]
                oacc_ref[rt, sl] = o
        mt_ref[...] = jnp.zeros_like(mt_ref)
        q16_ref[...] = jnp.zeros_like(q16_ref)

    rows = [pl.ds(pl.multiple_of(t * B + step * SBB, SBB), SBB) for t in range(L)]
    q_t = [qin_ref[rows[t], :] for t in range(L)]
    k_t = [kout_ref[rows[t], :] for t in range(L)]
    v_t = [v_ref[rows[t], :] for t in range(L)]
    e_t = eend_ref[pl.ds(pl.multiple_of(step * SBB, SBB), SBB), :]
    for bi in range(SBB):
        for h in range(HH):
            sl = slice(h * DK, (h + 1) * DK)
            s0 = s0_ref[0, bi, h]
            for t in range(L):
                q16_ref[t:t + 1, :] = q_t[t][bi:bi + 1, sl]
            oi = _dot(q16_ref[...].astype(bf16), s0.astype(bf16))
            for t in range(L):
                ostage_ref[t, bi:bi + 1, sl] = oi[t:t + 1]
            mt_ref[0:1, :] = e_t[bi:bi + 1, sl]
            for t in range(L):
                mt_ref[t + 1:t + 2, :] = k_t[t][bi:bi + 1, sl]
            cols = mt_ref[...].T
            sn = s0 * cols[:, 0:1]
            for t in range(L):
                sn = sn + cols[:, t + 1:t + 2] * v_t[t][bi:bi + 1, sl]
            snew_ref[bi, h] = sn
    for t in range(L):
        oacc_ref[rows[t], :] += ostage_ref[t]

    @pl.when(step == pl.num_programs(0) - 1)
    def _():
        gate = _silu(og_ref[...])
        ng = ng_ref[...]
        for h in range(HH):
            sl = slice(h * DK, (h + 1) * DK)
            o_ref[:, sl] = (_rms(oacc_ref[:, sl], ng) * gate[:, sl]).astype(bf16)


def _hgrn_sample(l, proj, lb, onorm_g, state):
    def slab(k):
        return pl.BlockSpec((TS, KW), lambda s: (TP // TS, k))

    return pl.pallas_call(
        _hgrn_sample_kernel,
        grid=(DEC_BATCH // SBB,),
        in_specs=[slab(OFF_HQ // KW), slab(OFF_HF // KW), slab(OFF_HV // KW), slab(OFF_HG // KW),
                  pl.BlockSpec((1, KW), lambda s: (0, 0)),
                  pl.BlockSpec((1, DV), lambda s: (0, 0)),
                  pl.BlockSpec((1, SBB, HH, DK, DV), lambda s: (l, s, 0, 0, 0))],
        out_specs=[pl.BlockSpec((TS, KW), lambda s: (0, 0)),
                   pl.BlockSpec((SBB, HH, DK, DV), lambda s: (s, 0, 0, 0))],
        out_shape=[jax.ShapeDtypeStruct((TS, KW), bf16),
                   jax.ShapeDtypeStruct((DEC_BATCH, HH, DK, DV), f32)],
        scratch_shapes=[pltpu.VMEM((TS, KW), f32), pltpu.VMEM((TS, KW), f32),
                        pltpu.VMEM((DEC_BATCH, KW), f32), pltpu.VMEM((TS, KW), f32),
                        pltpu.VMEM((LANES, DK), f32), pltpu.VMEM((2 * SUBLANES, DK), f32),
                        pltpu.VMEM((DEC_SEQ, SBB, KW), f32)],
        compiler_params=_cparams(("arbitrary",), 56),
        name="hgrn_sample",
    )(proj, proj, proj, proj, lb.reshape(1, KW), onorm_g.reshape(1, DV), state)


def _head_rms(x, g, n):
    return jnp.concatenate([_rms(x[:, i * HD:(i + 1) * HD], g) for i in range(n)], axis=1)


def _sink_softmax_pv(s, sink_col, vb):
    m = jnp.maximum(jnp.max(s, axis=-1, keepdims=True), sink_col)
    p = jnp.exp(s - m)
    den = jnp.sum(p, axis=-1, keepdims=True) + jnp.exp(sink_col - m)
    return _dot(p.astype(bf16), vb) / den


def _attn_prompt_kernel(sink_ref, q_ref, kc_ref, kp_ref, vc_ref, vp_ref, qg_ref, kg_ref, bias_ref,
                        o_ref, kn_ref):
    i = pl.program_id(1)
    kc = _head_rms(kc_ref[...], kg_ref[...], NKV)
    kp = _head_rms(kp_ref[...], kg_ref[...], NKV)
    kn_ref[...] = kc
    kk = jnp.concatenate([kp, kc], axis=0).astype(bf16)
    vv = jnp.concatenate([vp_ref[...], vc_ref[...]], axis=0).astype(bf16)
    q = q_ref[...]
    col = lax.broadcasted_iota(i32, (GQ * WINDOW, 2 * WINDOW), 1)
    valid = (i > 0) | (col >= WINDOW)
    for n in range(NKV):
        qn = jnp.concatenate(
            [_rms(q[:, (n * GQ + g) * HD:(n * GQ + g + 1) * HD], qg_ref[...]) for g in range(GQ)], axis=0)
        s = _dot_nt(qn.astype(bf16), kk[:, n * HD:(n + 1) * HD]) * (HD ** -0.5)
        s = s + bias_ref[n * GQ:(n + 1) * GQ].reshape(GQ * WINDOW, 2 * WINDOW)
        s = jnp.where(valid, s, NEG)
        sink_col = jnp.concatenate(
            [jnp.full((WINDOW, 1), sink_ref[n * GQ + g], f32) for g in range(GQ)], axis=0)
        o = _sink_softmax_pv(s, sink_col, vv[:, n * HD:(n + 1) * HD])
        for g in range(GQ):
            hh = n * GQ + g
            o_ref[:, hh * HD:(hh + 1) * HD] = o[g * WINDOW:(g + 1) * WINDOW].astype(bf16)


def _attn_prompt(proj, sinks, q_norm_g, k_norm_g, bias):
    nb = SEQ // WINDOW
    kblk, vblk = OFF_AK // KVW, OFF_AV // KVW

    def cur(col):
        return lambda b, i: (b * nb + i, col)

    def prev(col):
        return lambda b, i: (b * nb + jnp.maximum(i - 1, 0), col)

    return pl.pallas_call(
        _attn_prompt_kernel,
        grid=(BATCH, nb),
        in_specs=[pl.BlockSpec(memory_space=pltpu.SMEM),
                  pl.BlockSpec((WINDOW, NH * HD), cur(OFF_AQ // (NH * HD))),
                  pl.BlockSpec((WINDOW, KVW), cur(kblk)),
                  pl.BlockSpec((WINDOW, KVW), prev(kblk)),
                  pl.BlockSpec((WINDOW, KVW), cur(vblk)),
                  pl.BlockSpec((WINDOW, KVW), prev(vblk)),
                  pl.BlockSpec((1, HD), lambda b, i: (0, 0)),
                  pl.BlockSpec((1, HD), lambda b, i: (0, 0)),
                  pl.BlockSpec((NH, WINDOW, 2 * WINDOW), lambda b, i: (0, 0, 0))],
        out_specs=[pl.BlockSpec((WINDOW, NH * HD), lambda b, i: (b * nb + i, 0)),
                   pl.BlockSpec((WINDOW, KVW), lambda b, i: (b * nb + i, 0))],
        out_shape=[jax.ShapeDtypeStruct((TP, NH * HD), bf16),
                   jax.ShapeDtypeStruct((TP, KVW), f32)],
        compiler_params=_cparams(("arbitrary", "arbitrary"), 32),
        name="attn_prompt",
    )(sinks, proj, proj, proj, proj, proj, q_norm_g.reshape(1, HD), k_norm_g.reshape(1, HD), bias)


def _attn_sample_kernel(sink_ref, q_ref, k_ref, v_ref, ck_ref, cv_ref, qg_ref, kg_ref, bias_ref,
                        o_ref, kw_ref, vw_ref):
    pad_rows = 2 * WINDOW - WINDOW - SPAD
    for bi in range(SBB):
        kn = _head_rms(k_ref[:, bi, :], kg_ref[...], NKV)
        vn = v_ref[:, bi, :]
        ck = ck_ref[0, bi]
        cv = cv_ref[0, bi]
        kw_ref[bi, 0:WINDOW - DEC_SEQ, :] = ck[DEC_SEQ:, :]
        kw_ref[bi, WINDOW - DEC_SEQ:WINDOW, :] = kn[0:DEC_SEQ]
        vw_ref[bi, 0:WINDOW - DEC_SEQ, :] = cv[DEC_SEQ:, :]
        vw_ref[bi, WINDOW - DEC_SEQ:WINDOW, :] = vn[0:DEC_SEQ]
        zpad = jnp.zeros((pad_rows, KVW), f32)
        kk = jnp.concatenate([ck, kn, zpad], axis=0).astype(bf16)
        vv = jnp.concatenate([cv, vn, zpad], axis=0).astype(bf16)
        q = q_ref[:, bi, :]
        for n in range(NKV):
            qn = jnp.concatenate(
                [_rms(q[:, (n * GQ + g) * HD:(n * GQ + g + 1) * HD], qg_ref[...]) for g in range(GQ)], axis=0)
            s = _dot_nt(qn.astype(bf16), kk[:, n * HD:(n + 1) * HD]) * (HD ** -0.5)
            s = s + bias_ref[n * GQ:(n + 1) * GQ].reshape(GQ * SPAD, 2 * WINDOW)
            sink_col = jnp.concatenate(
                [jnp.full((SPAD, 1), sink_ref[n * GQ + g], f32) for g in range(GQ)], axis=0)
            o = _sink_softmax_pv(s, sink_col, vv[:, n * HD:(n + 1) * HD])
            for g in range(GQ):
                hh = n * GQ + g
                o_ref[:, bi, hh * HD:(hh + 1) * HD] = o[g * SPAD:(g + 1) * SPAD].astype(bf16)


def _attn_sample(l, qkv_pad, cache_k, cache_v, sinks, q_norm_g, k_norm_g, bias):
    kblk = NH * HD // KVW
    return pl.pallas_call(
        _attn_sample_kernel,
        grid=(DEC_BATCH // SBB,),
        in_specs=[pl.BlockSpec(memory_space=pltpu.SMEM),
                  pl.BlockSpec((SPAD, SBB, NH * HD), lambda s: (0, s, 0)),
                  pl.BlockSpec((SPAD, SBB, KVW), lambda s: (0, s, kblk)),
                  pl.BlockSpec((SPAD, SBB, KVW), lambda s: (0, s, kblk + 1)),
                  pl.BlockSpec((1, SBB, WINDOW, KVW), lambda s: (l, s, 0, 0)),
                  pl.BlockSpec((1, SBB, WINDOW, KVW), lambda s: (l, s, 0, 0)),
                  pl.BlockSpec((1, HD), lambda s: (0, 0)),
                  pl.BlockSpec((1, HD), lambda s: (0, 0)),
                  pl.BlockSpec((NH, SPAD, 2 * WINDOW), lambda s: (0, 0, 0))],
        out_specs=[pl.BlockSpec((SPAD, SBB, NH * HD), lambda s: (0, s, 0)),
                   pl.BlockSpec((SBB, WINDOW, KVW), lambda s: (s, 0, 0)),
                   pl.BlockSpec((SBB, WINDOW, KVW), lambda s: (s, 0, 0))],
        out_shape=[jax.ShapeDtypeStruct((SPAD, DEC_BATCH, NH * HD), bf16),
                   jax.ShapeDtypeStruct((DEC_BATCH, WINDOW, KVW), f32),
                   jax.ShapeDtypeStruct((DEC_BATCH, WINDOW, KVW), f32)],
        compiler_params=_cparams(("arbitrary",), 32),
        name="attn_sample",
    )(sinks, qkv_pad, qkv_pad, qkv_pad, cache_k, cache_v, q_norm_g.reshape(1, HD), k_norm_g.reshape(1, HD), bias)


def _merge_kernel(oh_ref, oa_ref, ga_ref, gb_ref, x_ref, g1p_ref, g1s_ref, wa_ref, wb_ref, wo_ref,
                  o_ref, acc_ref):
    i, j = pl.program_id(0), pl.program_id(1)
    merged = (jax.nn.sigmoid(ga_ref[...]) * _dot(oh_ref[...], wa_ref[0])
              + jax.nn.sigmoid(gb_ref[...]) * _dot(oa_ref[...], wb_ref[0]))
    part = _dot(merged.astype(bf16), wo_ref[0])

    @pl.when(j == 0)
    def _():
        acc_ref[...] = part

    @pl.when(j > 0)
    def _():
        acc_ref[...] += part

    @pl.when(j == pl.num_programs(1) - 1)
    def _():
        g1 = _mod_rows(i >= NPT, g1p_ref, g1s_ref, TM)
        o_ref[...] = x_ref[...] + g1 * acc_ref[...]


def _merge(l, o_h, o_a, proj, x, mod3, mod2, wa_bf, wb_bf, wo_bf):
    g1p, g1s = _mod_specs(2, TPB, NPT)
    return pl.pallas_call(
        _merge_kernel,
        grid=(NT, D // TN_MG),
        in_specs=[pl.BlockSpec((TM, KW), lambda i, j: (i, 0)),
                  pl.BlockSpec((TM, NH * HD), lambda i, j: (i, 0)),
                  pl.BlockSpec((TM, TN_MG), lambda i, j: (i, OFF_GA // TN_MG + j)),
                  pl.BlockSpec((TM, TN_MG), lambda i, j: (i, OFF_GB // TN_MG + j)),
                  pl.BlockSpec((TM, D), lambda i, j: (i, 0)),
                  g1p, g1s,
                  pl.BlockSpec((1, KW, TN_MG), lambda i, j: (l, 0, j)),
                  pl.BlockSpec((1, NH * HD, TN_MG), lambda i, j: (l, 0, j)),
                  pl.BlockSpec((1, TN_MG, D), lambda i, j: (l, j, 0))],
        out_specs=pl.BlockSpec((TM, D), lambda i, j: (i, 0)),
        out_shape=jax.ShapeDtypeStruct((T, D), f32),
        scratch_shapes=[pltpu.VMEM((TM, D), f32)],
        compiler_params=_cparams(("arbitrary", "arbitrary"), 48),
        name="merge",
    )(o_h, o_a, proj, proj, x, mod3, mod2, wa_bf, wb_bf, wo_bf)


def _first_index(hit, idx, big, axis):
    return jnp.min(jnp.where(hit, idx, big), axis=axis, keepdims=True)


def _pack_bf16_pairs(h):
    lo = lax.bitcast_convert_type(h[:, :D // 2].astype(bf16).astype(f32), u32)
    hi = lax.bitcast_convert_type(h[:, D // 2:].astype(bf16).astype(f32), u32)
    return (hi & jnp.uint32(0xFFFF0000)) | (lo >> 16)


def _unpack_bf16_pairs(p):
    lo = lax.bitcast_convert_type(p << 16, f32).astype(bf16)
    hi = lax.bitcast_convert_type(p & jnp.uint32(0xFFFF0000), f32).astype(bf16)
    return lo, hi


def _router_kernel(x_ref, g_ref, scp_ref, scs_ref, shp_ref, shs_ref, wr_ref, rb_ref,
                   hp_ref, ids_ref, wts_ref, rank_ref, cnt_ref):
    i = pl.program_id(0)
    y = _rms(x_ref[...], g_ref[...])
    sc = _mod_rows(i >= NPT, scp_ref, scs_ref, TM)
    sh = _mod_rows(i >= NPT, shp_ref, shs_ref, TM)
    h = y * (1.0 + sc) + sh
    hp_ref[...] = _pack_bf16_pairs(h).reshape(TM, ROW_TILES_X, LANES)
    wr = wr_ref[0]
    w1 = wr.astype(bf16)
    w2 = (wr - w1.astype(f32)).astype(bf16)
    h1 = h.astype(bf16)
    h2 = (h - h1.astype(f32)).astype(bf16)
    logits = _dot_nt(w1, h1) + (_dot_nt(w1, h2) + _dot_nt(w2, h1))
    scores = jax.nn.sigmoid(logits)
    biased = scores + rb_ref[...]
    b3 = biased.reshape(NG, EPG, TM)
    e_in = lax.broadcasted_iota(i32, (NG, EPG, TM), 1).astype(f32)
    m1 = jnp.max(b3, axis=1, keepdims=True)
    first = _first_index(b3 == m1, e_in, float(EPG), 1)
    m2 = jnp.max(jnp.where(e_in == first, -jnp.inf, b3), axis=1, keepdims=True)
    gs = (m1 + m2).reshape(NG, TM)
    g_idx = lax.broadcasted_iota(i32, (NG, TM), 0).astype(f32)
    gsel = jnp.zeros((NG, TM), f32)
    for _ in range(TOPKG):
        m = jnp.max(gs, axis=0, keepdims=True)
        pick = g_idx == _first_index(gs == m, g_idx, float(NG), 0)
        gsel = jnp.where(pick, 1.0, gsel)
        gs = jnp.where(pick, -jnp.inf, gs)
    emask = jnp.broadcast_to(gsel.reshape(NG, 1, TM), (NG, EPG, TM)).reshape(NE, TM)
    cur = jnp.where(emask > 0.5, biased, NEG)
    e_idx = lax.broadcasted_iota(i32, (NE, TM), 0).astype(f32)
    ids, ws, picks = [], [], []
    for _ in range(TOPK):
        m = jnp.max(cur, axis=0, keepdims=True)
        first = _first_index(cur == m, e_idx, float(NE), 0)
        pick = e_idx == first
        ids.append(first)
        picks.append(pick)
        ws.append(jnp.sum(jnp.where(pick, scores, 0.0), axis=0, keepdims=True))
        cur = jnp.where(pick, -jnp.inf, cur)
    w = jnp.concatenate(ws, axis=0)
    ids_ref[...] = jnp.concatenate(ids, axis=0).astype(i32)
    wts_ref[...] = w / jnp.sum(w, axis=0, keepdims=True) * ROUTED_SCALE

    @pl.when(i == 0)
    def _():
        cnt_ref[...] = jnp.zeros_like(cnt_ref)

    onehot = jnp.zeros((NE, TM), f32)
    for pick in picks:
        onehot = jnp.where(pick, 1.0, onehot)
    before = (lax.broadcasted_iota(i32, (TM, TM), 0) < lax.broadcasted_iota(i32, (TM, TM), 1)).astype(bf16)
    base = cnt_ref[:, 0:1] + _dot(onehot.astype(bf16), before)
    rank_ref[...] = jnp.concatenate(
        [jnp.sum(jnp.where(pick, base, 0.0), axis=0, keepdims=True) for pick in picks], axis=0).astype(i32)
    cnt_ref[...] += jnp.sum(onehot, axis=1, keepdims=True)


def _router(l, x1, norm_g, mod3, mod2, w_router_t, router_bias):
    scp, scs = _mod_specs(4, TPB, NPT)
    shp, shs = _mod_specs(3, TPB, NPT)
    return pl.pallas_call(
        _router_kernel,
        grid=(NT,),
        in_specs=[pl.BlockSpec((TM, D), lambda i: (i, 0)),
                  pl.BlockSpec((1, D), lambda i: (0, 0)),
                  scp, scs, shp, shs,
                  pl.BlockSpec((1, NE, D), lambda i: (l, 0, 0)),
                  pl.BlockSpec((NE, 1), lambda i: (0, 0))],
        out_specs=[pl.BlockSpec((TM, ROW_TILES_X, LANES), lambda i: (i, 0, 0)),
                   pl.BlockSpec((TOPK, TM), lambda i: (0, i)),
                   pl.BlockSpec((TOPK, TM), lambda i: (0, i)),
                   pl.BlockSpec((TOPK, TM), lambda i: (0, i)),
                   pl.BlockSpec((NE, LANES), lambda i: (0, 0))],
        out_shape=[jax.ShapeDtypeStruct((T, ROW_TILES_X, LANES), u32),
                   jax.ShapeDtypeStruct((TOPK, T), i32),
                   jax.ShapeDtypeStruct((TOPK, T), f32),
                   jax.ShapeDtypeStruct((TOPK, T), i32),
                   jax.ShapeDtypeStruct((NE, LANES), f32)],
        compiler_params=_cparams(("arbitrary",), 48),
        name="router",
    )(x1, norm_g.reshape(1, D), mod3, mod2, mod3, mod2, w_router_t, router_bias.reshape(NE, 1))


def _block_tables(cnt):
    counts = cnt[:, 0].astype(i32)
    padded = (counts + BLK - 1) // BLK * BLK
    pend = jnp.cumsum(padded)
    pstart = pend - padded
    blk_start = jnp.arange(NBLK, dtype=i32) * BLK
    block_e = jnp.minimum(jnp.sum(blk_start[:, None] >= pend[None, :], axis=1), NE - 1).astype(i32)
    n_used = (pend[-1] // BLK).astype(i32).reshape(1)
    return block_e, n_used, pstart.astype(i32), (pstart + counts).astype(i32), (padded - counts).astype(i32)


def _pos_kernel(ps_ref, ids_ref, rank_ref, pos_ref):
    ids = ids_ref[...]
    acc = rank_ref[...]
    for e in range(NE):
        acc = acc + jnp.where(ids == e, ps_ref[e], 0)
    pos_ref[...] = acc


def _slot_positions(pstart, ids_t, rank_t):
    return pl.pallas_call(
        _pos_kernel,
        grid=(1,),
        in_specs=[pl.BlockSpec(memory_space=pltpu.SMEM),
                  pl.BlockSpec((TOPK, T), lambda i: (0, 0)),
                  pl.BlockSpec((TOPK, T), lambda i: (0, 0))],
        out_specs=pl.BlockSpec((TOPK, T), lambda i: (0, 0)),
        out_shape=jax.ShapeDtypeStruct((TOPK, T), i32),
        compiler_params=_cparams(("arbitrary",), 16),
        name="slot_positions",
    )(pstart, ids_t, rank_t)


def _dispatch_kernel(pad0_ref, padn_ref, nu_ref, pos_ref, hp_hbm, xs_hbm, zero_ref, sem):
    i = pl.program_id(0)

    def tok_copy(k, j, u):
        r = j * SUBLANES + u
        return pltpu.make_async_copy(hp_hbm.at[i * TMF + r], xs_hbm.at[pos_ref[0, 0, k * TMF + r]], sem)

    def pad_copy(e, r):
        return pltpu.make_async_copy(zero_ref.at[0], xs_hbm.at[pad0_ref[e] + r], sem)

    def tail_copy(b):
        return pltpu.make_async_copy(zero_ref, xs_hbm.at[pl.ds(b * BLK, BLK)], sem)

    def for_tokens(fn):
        for k in range(TOPK):
            def body(j, c):
                for u in range(SUBLANES):
                    fn(tok_copy(k, j, u), u)
                return c
            lax.fori_loop(0, TMF // SUBLANES, body, 0)

    def for_pads(fn):
        def per_expert(e, c):
            def body(r, c2):
                fn(pad_copy(e, r), 0)
                return c2
            return lax.fori_loop(0, padn_ref[e], body, c)
        lax.fori_loop(0, NE, per_expert, 0)

        def per_block(b, c):
            fn(tail_copy(b), 0)
            return c
        lax.fori_loop(nu_ref[0], NBLK, per_block, 0)

    @pl.when(i == 0)
    def _():
        zero_ref[...] = jnp.zeros_like(zero_ref)
        for_pads(lambda cp, u: cp.start(priority=u % 2))

    for_tokens(lambda cp, u: cp.start(priority=u % 2))
    for_tokens(lambda cp, u: cp.wait())

    @pl.when(i == 0)
    def _():
        for_pads(lambda cp, u: cp.wait())


def _dispatch(pad_start, pad_cnt, n_used, pos_tiles, hp):
    grid_spec = pltpu.PrefetchScalarGridSpec(
        num_scalar_prefetch=3,
        grid=(T // TMF,),
        in_specs=[pl.BlockSpec((1, 1, TOPK * TMF), lambda i, p0, pn, nu: (i, 0, 0), memory_space=pltpu.SMEM),
                  pl.BlockSpec(memory_space=pl.ANY)],
        out_specs=pl.BlockSpec(memory_space=pl.ANY),
        scratch_shapes=[pltpu.VMEM((BLK, ROW_TILES_X, LANES), u32), pltpu.SemaphoreType.DMA(())],
    )
    return pl.pallas_call(
        _dispatch_kernel,
        grid_spec=grid_spec,
        out_shape=jax.ShapeDtypeStruct((NROW, ROW_TILES_X, LANES), u32),
        compiler_params=_cparams(("arbitrary",), 16),
        name="dispatch",
    )(pad_start, pad_cnt, n_used, pos_tiles, hp)


def _expert_kernel(be_ref, nu_ref, x_ref, wg_ref, wu_ref, wd_ref, y_ref, wgb, wub, wdb):
    i = pl.program_id(0)

    @pl.when(i >= nu_ref[0])
    def _():
        y_ref[...] = jnp.zeros_like(y_ref)

    @pl.when(i < nu_ref[0])
    def _():
        @pl.when((i == 0) | (be_ref[i] != be_ref[jnp.maximum(i - 1, 0)]))
        def _():
            wgb[...] = wg_ref[0, 0].astype(bf16)
            wub[...] = wu_ref[0, 0].astype(bf16)
            wdb[...] = wd_ref[0, 0].astype(bf16)

        lo, hi = _unpack_bf16_pairs(x_ref[...].reshape(BLK, D // 2))
        half = D // 2
        g = _dot(lo, wgb[0:half, :]) + _dot(hi, wgb[half:D, :])
        u = _dot(lo, wub[0:half, :]) + _dot(hi, wub[half:D, :])
        y_ref[...] = _dot((_silu(g) * u).astype(bf16), wdb[...]).reshape(BLK, ROW_TILES_Y, LANES)


def _experts(l, xs, block_e, n_used, w_gate, w_up, w_down):
    def blk(i, be, nu):
        return jnp.minimum(i, nu[0] - 1)

    grid_spec = pltpu.PrefetchScalarGridSpec(
        num_scalar_prefetch=2,
        grid=(NBLK,),
        in_specs=[pl.BlockSpec((BLK, ROW_TILES_X, LANES), lambda i, be, nu: (blk(i, be, nu), 0, 0)),
                  pl.BlockSpec((1, 1, D, DE), lambda i, be, nu: (l, be[blk(i, be, nu)], 0, 0)),
                  pl.BlockSpec((1, 1, D, DE), lambda i, be, nu: (l, be[blk(i, be, nu)], 0, 0)),
                  pl.BlockSpec((1, 1, DE, D), lambda i, be, nu: (l, be[blk(i, be, nu)], 0, 0))],
        out_specs=pl.BlockSpec((BLK, ROW_TILES_Y, LANES), lambda i, be, nu: (i, 0, 0)),
        scratch_shapes=[pltpu.VMEM((D, DE), bf16), pltpu.VMEM((D, DE), bf16), pltpu.VMEM((DE, D), bf16)],
    )
    return pl.pallas_call(
        _expert_kernel,
        grid_spec=grid_spec,
        out_shape=jax.ShapeDtypeStruct((NROW, ROW_TILES_Y, LANES), f32),
        compiler_params=_cparams(("arbitrary",), 56),
        name="experts",
    )(block_e, n_used, xs, w_gate, w_up, w_down)


def _final_kernel(pos_ref, posn_ref, x_ref, hp_ref, w_ref, g2p_ref, g2s_ref, sg_ref, su_ref, sd_ref,
                  y_hbm, o_ref, ybuf, sem):
    i = pl.program_id(0)
    n = pl.num_programs(0)
    slot = i % 2

    def row_copy(pref, k, j, u, s):
        r = j * SUBLANES + u
        return pltpu.make_async_copy(y_hbm.at[pref[0, 0, k * TMF + r]], ybuf.at[s, k, r], sem.at[s])

    def for_rows(pref, s, fn):
        for k in range(TOPK):
            def body(j, c):
                for u in range(SUBLANES):
                    fn(row_copy(pref, k, j, u, s), u)
                return c
            lax.fori_loop(0, TMF // SUBLANES, body, 0)

    @pl.when(i == 0)
    def _():
        for_rows(pos_ref, 0, lambda cp, u: cp.start(priority=u % 2))

    @pl.when(i + 1 < n)
    def _():
        for_rows(posn_ref, 1 - slot, lambda cp, u: cp.start(priority=u % 2))

    lo, hi = _unpack_bf16_pairs(hp_ref[...].reshape(TMF, D // 2))
    hb = jnp.concatenate([lo, hi], axis=1)
    shared = _dot((_silu(_dot(hb, sg_ref[0])) * _dot(hb, su_ref[0])).astype(bf16), sd_ref[0])

    for_rows(pos_ref, slot, lambda cp, u: cp.wait())

    w = w_ref[...]
    routed = w[:, 0:1] * ybuf[slot, 0].reshape(TMF, D)
    for k in range(1, TOPK):
        routed = routed + w[:, k:k + 1] * ybuf[slot, k].reshape(TMF, D)
    g2 = _mod_rows(i >= TP // TMF, g2p_ref, g2s_ref, TMF)
    o_ref[...] = x_ref[...] + g2 * (shared + routed)


def _final(l, x1, hp, wts, pos_tiles, mod3, mod2, sg_bf, su_bf, sd_bf, yb):
    g2p, g2s = _mod_specs(5, SEQ // TMF, TP // TMF)
    ntile = T // TMF
    return pl.pallas_call(
        _final_kernel,
        grid=(ntile,),
        in_specs=[pl.BlockSpec((1, 1, TOPK * TMF), lambda i: (i, 0, 0), memory_space=pltpu.SMEM),
                  pl.BlockSpec((1, 1, TOPK * TMF), lambda i: (jnp.minimum(i + 1, ntile - 1), 0, 0),
                               memory_space=pltpu.SMEM),
                  pl.BlockSpec((TMF, D), lambda i: (i, 0)),
                  pl.BlockSpec((TMF, ROW_TILES_X, LANES), lambda i: (i, 0, 0)),
                  pl.BlockSpec((TMF, TOPK), lambda i: (i, 0)),
                  g2p, g2s,
                  pl.BlockSpec((1, D, DE), lambda i: (l, 0, 0)),
                  pl.BlockSpec((1, D, DE), lambda i: (l, 0, 0)),
                  pl.BlockSpec((1, DE, D), lambda i: (l, 0, 0)),
                  pl.BlockSpec(memory_space=pl.ANY)],
        out_specs=pl.BlockSpec((TMF, D), lambda i: (i, 0)),
        out_shape=jax.ShapeDtypeStruct((T, D), f32),
        scratch_shapes=[pltpu.VMEM((2, TOPK, TMF, ROW_TILES_Y, LANES), f32),
                        pltpu.SemaphoreType.DMA((2,))],
        compiler_params=_cparams(("arbitrary",), 56),
        name="moe_combine",
    )(pos_tiles, pos_tiles, x1, hp, wts, mod3, mod2, sg_bf, su_bf, sd_bf, yb)


def _layer(l, x, mod, lb, state, cache_k, cache_v, bias_p, bias_s, norm1_g, norm2_g, w_in_bf, onorm_g,
           q_norm_g, k_norm_g, sinks, wa_bf, wb_bf, wo_bf, w_router_t, router_bias,
           w_eg, w_eu, w_ed, sg_bf, su_bf, sd_bf):
    mod3 = mod.reshape(mod.shape[0], 1, 6 * D)
    proj = _in_proj(l, x, norm1_g, mod3, mod, w_in_bf)

    oh_p, s_p = _hgrn_prompt(proj, lb, onorm_g)
    oh_s, s_s = _hgrn_sample(l, proj, lb, onorm_g, state)

    oa_p, kn_p = _attn_prompt(proj, sinks, q_norm_g, k_norm_g, bias_p)
    qkv_pad = jnp.pad(proj[TP:, OFF_AQ:OFF_GA].reshape(DEC_SEQ, DEC_BATCH, OFF_GA - OFF_AQ),
                      ((0, SPAD - DEC_SEQ), (0, 0), (0, 0)))
    oa_s, kw_s, vw_s = _attn_sample(l, qkv_pad, cache_k, cache_v, sinks, q_norm_g, k_norm_g, bias_s)

    o_h = jnp.concatenate([oh_p, oh_s], axis=0)
    o_a = jnp.concatenate([oa_p, oa_s[:DEC_SEQ].reshape(TS, NH * HD)], axis=0)
    x1 = _merge(l, o_h, o_a, proj, x, mod3, mod, wa_bf, wb_bf, wo_bf)

    hp, ids_t, wts_t, rank_t, cnt = _router(l, x1, norm2_g, mod3, mod, w_router_t, router_bias)
    block_e, n_used, pstart, pad_start, pad_cnt = _block_tables(cnt)
    pos = _slot_positions(pstart, ids_t, rank_t)
    pos_tiles = pos.reshape(TOPK, T // TMF, TMF).transpose(1, 0, 2).reshape(T // TMF, 1, TOPK * TMF)
    xs = _dispatch(pad_start, pad_cnt, n_used, pos_tiles, hp)
    yb = _experts(l, xs, block_e, n_used, w_eg, w_eu, w_ed)
    x2 = _final(l, x1, hp, wts_t.T, pos_tiles, mod3, mod, sg_bf, su_bf, sd_bf, yb)

    k_p = kn_p.reshape(BATCH, SEQ, NKV, HD)[:, SEQ - WINDOW:]
    v_p = proj[:TP, OFF_AV:OFF_GA].reshape(BATCH, SEQ, NKV, HD)[:, SEQ - WINDOW:]
    return (x2, s_p, s_s, k_p, v_p, kw_s.reshape(DEC_BATCH, WINDOW, NKV, HD),
            vw_s.reshape(DEC_BATCH, WINDOW, NKV, HD))


def kernel(x_prompt, x_sample, state_hgrn, cache_k_win, cache_v_win, c_prompt, c_sample, norm1_g, norm2_g,
           w_ada, b_ada, w_in, hgrn_lb_logits, hgrn_onorm_g, q_norm_g, k_norm_g, attn_sinks, rel_bias,
           w_branch_a, w_branch_b, w_out, w_router, router_bias, w_exp_gate, w_exp_up, w_exp_down,
           w_sh_gate, w_sh_up, w_sh_down):
    p_lb = jax.nn.softmax(hgrn_lb_logits.astype(f32), axis=0)
    lower_bounds = jnp.clip(jnp.cumsum(p_lb, axis=0) - p_lb[0], 0.0, 1.0 - 1e-6)

    c_all = jnp.concatenate([c_sample, c_prompt, jnp.zeros((SUBLANES - BATCH, D), f32)], axis=0)
    mod = _ada_mod(c_all, w_ada, b_ada)
    bias_p = _bias_table(rel_bias, _prompt_buckets())
    bias_s = _bias_table(rel_bias, _sample_buckets())

    x = jnp.concatenate([x_prompt.reshape(TP, D), x_sample.transpose(1, 0, 2).reshape(TS, D)], axis=0)
    cache_k = cache_k_win.reshape(DEPTH, DEC_BATCH, WINDOW, KVW)
    cache_v = cache_v_win.reshape(DEPTH, DEC_BATCH, WINDOW, KVW)
    w_in_bf, wa_bf, wb_bf, wo_bf = (w.astype(bf16) for w in (w_in, w_branch_a, w_branch_b, w_out))
    sg_bf, su_bf, sd_bf = (w.astype(bf16) for w in (w_sh_gate, w_sh_up, w_sh_down))
    w_router_t = w_router.transpose(0, 2, 1)
    sp_l, ss_l, kp_l, vp_l, ks_l, vs_l = [], [], [], [], [], []
    for l in range(DEPTH):
        x, s_p, s_s, k_p, v_p, k_s, v_s = _layer(
            l, x, mod[l], lower_bounds[l], state_hgrn, cache_k, cache_v, bias_p, bias_s,
            norm1_g[l], norm2_g[l], w_in_bf, hgrn_onorm_g[l], q_norm_g[l], k_norm_g[l], attn_sinks[l],
            wa_bf, wb_bf, wo_bf, w_router_t, router_bias[l],
            w_exp_gate, w_exp_up, w_exp_down, sg_bf, su_bf, sd_bf)
        sp_l.append(s_p); ss_l.append(s_s); kp_l.append(k_p); vp_l.append(v_p); ks_l.append(k_s); vs_l.append(v_s)

    y_prompt = x[:TP].reshape(BATCH, SEQ, D)
    y_sample = x[TP:].reshape(DEC_SEQ, DEC_BATCH, D).transpose(1, 0, 2)
    return (y_prompt, y_sample, jnp.stack(sp_l), jnp.stack(ss_l), jnp.stack(kp_l), jnp.stack(vp_l),
            jnp.stack(ks_l), jnp.stack(vs_l))
```

```python
import functools
import math

import numpy as np
import jax
import jax.numpy as jnp
from jax import lax
from jax.experimental import pallas as pl
from jax.experimental.pallas import tpu as pltpu

f32 = jnp.float32
bf16 = jnp.bfloat16
i32 = jnp.int32
u32 = jnp.uint32

D = 2048
BATCH, SEQ = 2, 4096
DEPTH = 2
DEC_BATCH, DEC_SEQ = 128, 4
HH, DK, DV = 8, 128, 128
KW = HH * DK
NH, NKV, HD = 16, 4, 64
GQ = NH // NKV
KVW = NKV * HD
WINDOW = 128
NUM_BUCKETS, MAX_DISTANCE = 32, 128
NE, NG, TOPKG, TOPK, DE = 64, 8, 4, 8, 512
EPG = NE // NG
ROUTED_SCALE = 2.5
IN_WIDTH = 2 * KW + 2 * KW + NH * HD + 2 * KVW + 2 * D
EPS = 1e-6
NEG = -1e30
TINY = 1e-30

OFF_HQ, OFF_HF, OFF_HV, OFF_HG = 0, KW, 2 * KW, 3 * KW
OFF_AQ = 4 * KW
OFF_AK = OFF_AQ + NH * HD
OFF_AV = OFF_AK + KVW
OFF_GA = OFF_AV + KVW
OFF_GB = OFF_GA + D

TP = BATCH * SEQ
TS = DEC_BATCH * DEC_SEQ
T = TP + TS
TM = 512
NPT = TP // TM
NT = T // TM
TPB = SEQ // TM
TN_IN = 2432
TN_MG = 512
TMF = 128
HC = 256
HGRN_WIDTHS = (128, 64, 32, 16, 8, 4)
SBB = 8
SPAD = 8
BLK = 256
NA = T * TOPK
NBLK = NA // BLK + NE
NROW = NBLK * BLK

V7X_VMEM_BYTES = 64 * 1024 * 1024
SUBLANES, LANES = 8, 128
ROW_TILES_X = D // 2 // LANES


def _cparams(sem, vmem_mib):
    assert vmem_mib * 1024 * 1024 < V7X_VMEM_BYTES
    return pltpu.CompilerParams(dimension_semantics=sem, vmem_limit_bytes=vmem_mib * 1024 * 1024)


def _silu(x):
    return x * jax.nn.sigmoid(x)


def _rms(x, g):
    return x * lax.rsqrt(jnp.mean(x * x, axis=-1, keepdims=True) + EPS) * g


def _dot(a, b):
    return jnp.dot(a, b, preferred_element_type=f32)


def _dot_nt(a, b):
    return lax.dot_general(a, b, (((1,), (1,)), ((), ())), preferred_element_type=f32)


def _dot_tn(a, b):
    return lax.dot_general(a, b, (((0,), (0,)), ((), ())), preferred_element_type=f32)


def _split3(x):
    a = x.astype(bf16)
    r = x - a.astype(f32)
    b = r.astype(bf16)
    c = (r - b.astype(f32)).astype(bf16)
    return a, b, c


def _mod_rows(is_sample, p_ref, s_ref, rows):
    p = p_ref[0]
    s = s_ref[...]
    if rows != DEC_BATCH:
        s = jnp.broadcast_to(s[None], (rows // DEC_BATCH, DEC_BATCH, D)).reshape(rows, D)
    return jnp.where(is_sample, s, p)


def _mod_specs(chunk, tiles_per_seq, n_prompt_tiles, extra_args=0):
    def pmap(i, *_):
        return (DEC_BATCH + jnp.minimum(i // tiles_per_seq, BATCH - 1), 0, chunk)

    def smap(i, *_):
        return (0, chunk)

    return (pl.BlockSpec((1, 1, D), pmap), pl.BlockSpec((DEC_BATCH, D), smap))


def _ada_kernel(c_ref, w_ref, b_ref, o_ref):
    c = c_ref[...]
    a = _silu(c).astype(bf16)
    o_ref[0] = _dot(a, w_ref[0].astype(bf16)) + b_ref[0]


def _ada_mod(c_all, w_ada, b_ada):
    rows = c_all.shape[0]
    tn = 1024
    return pl.pallas_call(
        _ada_kernel,
        grid=(DEPTH, 6 * D // tn),
        in_specs=[pl.BlockSpec((rows, D), lambda l, j: (0, 0)),
                  pl.BlockSpec((1, D, tn), lambda l, j: (l, 0, j)),
                  pl.BlockSpec((1, 1, tn), lambda l, j: (l, 0, j))],
        out_specs=pl.BlockSpec((1, rows, tn), lambda l, j: (l, 0, j)),
        out_shape=jax.ShapeDtypeStruct((DEPTH, rows, 6 * D), f32),
        compiler_params=_cparams(("arbitrary", "arbitrary"), 40),
        name="ada_mod",
    )(c_all, w_ada, b_ada.reshape(DEPTH, 1, 6 * D))


def _bucket_np(dist):
    n = np.maximum(dist, 0)
    exact = NUM_BUCKETS // 2
    nf = np.maximum(n, 1).astype(np.float32)
    large = exact + (np.log(nf / np.float32(exact)) / np.float32(math.log(MAX_DISTANCE / exact))
                     * np.float32(NUM_BUCKETS - exact)).astype(np.int32)
    return np.where(n < exact, n, np.clip(large, exact, NUM_BUCKETS - 1)).astype(np.int32)


def _prompt_buckets():
    j = np.arange(2 * WINDOW)
    dist = np.arange(WINDOW)[:, None] + WINDOW - j[None, :]
    band = (dist >= 0) & (dist < WINDOW)
    return np.where(band, _bucket_np(dist), -1).astype(np.int32)


def _sample_buckets():
    l = np.arange(SPAD)[:, None]
    j = np.arange(2 * WINDOW)[None, :]
    dist = WINDOW + l - j
    ok = (dist >= 0) & (dist < WINDOW) & (l < DEC_SEQ) & (j < WINDOW + DEC_SEQ)
    return np.where(ok, _bucket_np(dist), -1).astype(np.int32)


def _bias_kernel(tab_ref, bk_ref, o_ref):
    h = pl.program_id(0)
    bk = bk_ref[...]
    acc = jnp.zeros(bk.shape, f32)
    for i in range(NUM_BUCKETS):
        acc = jnp.where(bk == i, tab_ref[i, h], acc)
    o_ref[0] = jnp.where(bk < 0, NEG, acc)


def _bias_table(rel_bias, buckets):
    r, c = buckets.shape
    return pl.pallas_call(
        _bias_kernel,
        grid=(NH,),
        in_specs=[pl.BlockSpec(memory_space=pltpu.SMEM),
                  pl.BlockSpec((r, c), lambda h: (0, 0))],
        out_specs=pl.BlockSpec((1, r, c), lambda h: (h, 0, 0)),
        out_shape=jax.ShapeDtypeStruct((NH, r, c), f32),
        compiler_params=_cparams(("arbitrary",), 16),
        name="bias_table",
    )(rel_bias, jnp.asarray(buckets))


def _inproj_kernel(x_ref, g_ref, scp_ref, scs_ref, shp_ref, shs_ref, w_ref, o_ref, h_ref):
    i = pl.program_id(0)

    @pl.when(pl.program_id(1) == 0)
    def _():
        y = _rms(x_ref[...], g_ref[...])
        sc = _mod_rows(i >= NPT, scp_ref, scs_ref, TM)
        sh = _mod_rows(i >= NPT, shp_ref, shs_ref, TM)
        h_ref[...] = (y * (1.0 + sc) + sh).astype(bf16)

    o_ref[...] = _dot(h_ref[...], w_ref[0])


def _in_proj(l, x, norm_g, mod3, mod2, w_in_bf):
    scp, scs = _mod_specs(1, TPB, NPT)
    shp, shs = _mod_specs(0, TPB, NPT)
    return pl.pallas_call(
        _inproj_kernel,
        grid=(NT, IN_WIDTH // TN_IN),
        in_specs=[pl.BlockSpec((TM, D), lambda i, j: (i, 0)),
                  pl.BlockSpec((1, D), lambda i, j: (0, 0)),
                  scp, scs, shp, shs,
                  pl.BlockSpec((1, D, TN_IN), lambda i, j: (l, 0, j))],
        out_specs=pl.BlockSpec((TM, TN_IN), lambda i, j: (i, j)),
        out_shape=jax.ShapeDtypeStruct((T, IN_WIDTH), f32),
        scratch_shapes=[pltpu.VMEM((TM, D), bf16)],
        compiler_params=_cparams(("arbitrary", "arbitrary"), 56),
        name="in_proj",
    )(x, norm_g.reshape(1, D), mod3, mod2, mod3, mod2, w_in_bf)


def _hgrn_gates(z, lb):
    f = lb + (1.0 - lb) * jax.nn.sigmoid(z)
    g = jnp.log(jnp.maximum(f, TINY))
    k = (1.0 - lb) * jax.nn.sigmoid(-z)
    return g, k


def _bcast_block_row(b, period, row):
    c, w = b.shape
    b3 = b.reshape(c // period, period, w)
    return jnp.broadcast_to(b3[:, row:row + 1, :], b3.shape).reshape(c, w)


def _hgrn_level_table():
    t = np.arange(HC)[:, None]
    s = np.arange(HC)[None, :]
    lev = np.full((HC, HC), -1, np.int32)
    for li, w in enumerate(HGRN_WIDTHS + (2,)):
        m = (t // (2 * w) == s // (2 * w)) & ((t // w) % 2 == 1) & ((s // w) % 2 == 0)
        lev[m] = li
    lev[(t // 2 == s // 2) & (s <= t)] = len(HGRN_WIDTHS) + 1
    return lev


def _hgrn_prompt_kernel(q_ref, f_ref, v_ref, og_ref, lb_ref, ng_ref, lev_ref, o_ref, sfin_ref, st_ref):
    c = pl.program_id(1)

    @pl.when(c == 0)
    def _():
        st_ref[...] = jnp.zeros_like(st_ref)

    lb = lb_ref[...]
    g, k = _hgrn_gates(f_ref[...], lb)
    qh = _silu(q_ref[...])
    vb = v_ref[...].astype(bf16)
    r = lax.broadcasted_iota(i32, (HC, HC), 0)
    s = lax.broadcasted_iota(i32, (HC, HC), 1)
    tri = (r >= s).astype(bf16)
    g1, g2, g3 = _split3(g)
    b = _dot(tri, g1) + _dot(tri, g2) + _dot(tri, g3)
    lev = lev_ref[...]

    qs, ks = [], []
    for w in HGRN_WIDTHS:
        m = _bcast_block_row(b, 2 * w, w - 1)
        qs.append((qh * jnp.exp(jnp.minimum(b - m, 0.0))).astype(bf16))
        ks.append((k * jnp.exp(jnp.minimum(m - b, 0.0))).astype(bf16))
    row = lax.broadcasted_iota(i32, (HC, 1), 0)
    g_prev = pltpu.roll(g, 1, 0)
    g_next = pltpu.roll(g, HC - 1, 0)
    r4 = row & 3
    qs.append((qh * jnp.exp(jnp.where(r4 == 2, g, jnp.where(r4 == 3, g + g_prev, 0.0)))).astype(bf16))
    ks.append((k * jnp.exp(jnp.where(r4 == 0, g_next, 0.0))).astype(bf16))
    odd = (row & 1) == 1
    qs.append((qh * jnp.exp(jnp.where(odd, g, 0.0))).astype(bf16))
    ks.append((k * jnp.exp(jnp.where(odd, -g, 0.0))).astype(bf16))
    b_end = b[HC - 1:HC, :]
    q_in = (qh * jnp.exp(b)).astype(bf16)
    k_out = (k * jnp.exp(b_end - b)).astype(bf16)
    e_end = jnp.exp(b_end)
    gate = _silu(og_ref[...])
    ng = ng_ref[...]

    for h in range(HH):
        sl = slice(h * DK, (h + 1) * DK)
        a = jnp.zeros((HC, HC), f32)
        for li in range(len(qs)):
            a = jnp.where(lev == li, _dot_nt(qs[li][:, sl], ks[li][:, sl]), a)
        st = st_ref[h]
        o = _dot(a.astype(bf16), vb[:, sl]) + _dot_nt(q_in[:, sl], st.astype(bf16))
        st_new = st * e_end[:, sl] + _dot_tn(vb[:, sl], k_out[:, sl])
        st_ref[h] = st_new
        o_ref[:, sl] = (_rms(o, ng) * gate[:, sl]).astype(bf16)

        @pl.when(c == pl.num_programs(1) - 1)
        def _():
            sfin_ref[0, h] = st_new.T


def _hgrn_prompt(proj, lb, onorm_g):
    nc = SEQ // HC

    def slab(k):
        return pl.BlockSpec((HC, KW), lambda b, c: (b * nc + c, k))

    return pl.pallas_call(
        _hgrn_prompt_kernel,
        grid=(BATCH, nc),
        in_specs=[slab(OFF_HQ // KW), slab(OFF_HF // KW), slab(OFF_HV // KW), slab(OFF_HG // KW),
                  pl.BlockSpec((1, KW), lambda b, c: (0, 0)),
                  pl.BlockSpec((1, DV), lambda b, c: (0, 0)),
                  pl.BlockSpec((HC, HC), lambda b, c: (0, 0))],
        out_specs=[pl.BlockSpec((HC, KW), lambda b, c: (b * nc + c, 0)),
                   pl.BlockSpec((1, HH, DK, DV), lambda b, c: (b, 0, 0, 0))],
        out_shape=[jax.ShapeDtypeStruct((TP, KW), bf16),
                   jax.ShapeDtypeStruct((BATCH, HH, DK, DV), f32)],
        scratch_shapes=[pltpu.VMEM((HH, DV, DK), f32)],
        compiler_params=_cparams(("arbitrary", "arbitrary"), 48),
        name="hgrn_prompt",
    )(proj, proj, proj, proj, lb.reshape(1, KW), onorm_g.reshape(1, DV), jnp.asarray(_hgrn_level_table()))


def _hgrn_sample_kernel(q_ref, f_ref, v_ref, og_ref, lb_ref, ng_ref, s0_ref, o_ref, snew_ref,
                        qin_ref, kout_ref, eend_ref, oacc_ref, mt_ref, q16_ref, ostage_ref):
    step = pl.program_id(0)
    L, B = DEC_SEQ, DEC_BATCH

    @pl.when(step == 0)
    def _():
        lb = lb_ref[...]
        g, k = _hgrn_gates(f_ref[...], lb)
        qh = _silu(q_ref[...])
        v = v_ref[...]
        bs = []
        acc = None
        for t in range(L):
            gt = g[t * B:(t + 1) * B]
            acc = gt if acc is None else acc + gt
            bs.append(acc)
        b_end = bs[-1]
        eend_ref[...] = jnp.exp(b_end)
        for t in range(L):
            rt = slice(t * B, (t + 1) * B)
            qin_ref[rt, :] = qh[rt] * jnp.exp(bs[t])
            kout_ref[rt, :] = k[rt] * jnp.exp(b_end - bs[t])
            for h in range(HH):
                sl = slice(h * DK, (h + 1) * DK)
                o = jnp.zeros((B, DV), f32)
                for s_ in range(t + 1):
                    rs = slice(s_ * B, (s_ + 1) * B)
                    w = jnp.sum(qh[rt, sl] * k[rs, sl] * jnp.exp(bs[t][:, sl] - bs[s_][:, sl]),
                                axis=-1, keepdims=True)
                    o = o + w * v[rs, sl]
                oacc_ref[rt, sl] = o
        mt_ref[...] = jnp.zeros_like(mt_ref)
        q16_ref[...] = jnp.zeros_like(q16_ref)

    rows = [pl.ds(pl.multiple_of(t * B + step * SBB, SBB), SBB) for t in range(L)]
    q_t = [qin_ref[rows[t], :] for t in range(L)]
    k_t = [kout_ref[rows[t], :] for t in range(L)]
    v_t = [v_ref[rows[t], :] for t in range(L)]
    e_t = eend_ref[pl.ds(pl.multiple_of(step * SBB, SBB), SBB), :]
    for bi in range(SBB):
        for h in range(HH):
            sl = slice(h * DK, (h + 1) * DK)
            s0 = s0_ref[0, bi, h]
            for t in range(L):
                q16_ref[t:t + 1, :] = q_t[t][bi:bi + 1, sl]
            oi = _dot(q16_ref[...].astype(bf16), s0.astype(bf16))
            for t in range(L):
                ostage_ref[t, bi:bi + 1, sl] = oi[t:t + 1]
            mt_ref[0:1, :] = e_t[bi:bi + 1, sl]
            for t in range(L):
                mt_ref[t + 1:t + 2, :] = k_t[t][bi:bi + 1, sl]
            cols = mt_ref[...].T
            sn = s0 * cols[:, 0:1]
            for t in range(L):
                sn = sn + cols[:, t + 1:t + 2] * v_t[t][bi:bi + 1, sl]
            snew_ref[bi, h] = sn
    for t in range(L):
        oacc_ref[rows[t], :] += ostage_ref[t]

    @pl.when(step == pl.num_programs(0) - 1)
    def _():
        gate = _silu(og_ref[...])
        ng = ng_ref[...]
        for h in range(HH):
            sl = slice(h * DK, (h + 1) * DK)
            o_ref[:, sl] = (_rms(oacc_ref[:, sl], ng) * gate[:, sl]).astype(bf16)


def _hgrn_sample(l, proj, lb, onorm_g, state):
    def slab(k):
        return pl.BlockSpec((TS, KW), lambda s: (TP // TS, k))

    return pl.pallas_call(
        _hgrn_sample_kernel,
        grid=(DEC_BATCH // SBB,),
        in_specs=[slab(OFF_HQ // KW), slab(OFF_HF // KW), slab(OFF_HV // KW), slab(OFF_HG // KW),
                  pl.BlockSpec((1, KW), lambda s: (0, 0)),
                  pl.BlockSpec((1, DV), lambda s: (0, 0)),
                  pl.BlockSpec((1, SBB, HH, DK, DV), lambda s: (l, s, 0, 0, 0))],
        out_specs=[pl.BlockSpec((TS, KW), lambda s: (0, 0)),
                   pl.BlockSpec((SBB, HH, DK, DV), lambda s: (s, 0, 0, 0))],
        out_shape=[jax.ShapeDtypeStruct((TS, KW), bf16),
                   jax.ShapeDtypeStruct((DEC_BATCH, HH, DK, DV), f32)],
        scratch_shapes=[pltpu.VMEM((TS, KW), f32), pltpu.VMEM((TS, KW), f32),
                        pltpu.VMEM((DEC_BATCH, KW), f32), pltpu.VMEM((TS, KW), f32),
                        pltpu.VMEM((LANES, DK), f32), pltpu.VMEM((2 * SUBLANES, DK), f32),
                        pltpu.VMEM((DEC_SEQ, SBB, KW), f32)],
        compiler_params=_cparams(("arbitrary",), 56),
        name="hgrn_sample",
    )(proj, proj, proj, proj, lb.reshape(1, KW), onorm_g.reshape(1, DV), state)


def _head_rms(x, g, n):
    return jnp.concatenate([_rms(x[:, i * HD:(i + 1) * HD], g) for i in range(n)], axis=1)


def _sink_softmax_pv(s, sink_col, vb):
    m = jnp.maximum(jnp.max(s, axis=-1, keepdims=True), sink_col)
    p = jnp.exp(s - m)
    den = jnp.sum(p, axis=-1, keepdims=True) + jnp.exp(sink_col - m)
    return _dot(p.astype(bf16), vb) / den


def _attn_prompt_kernel(sink_ref, q_ref, kc_ref, kp_ref, vc_ref, vp_ref, qg_ref, kg_ref, bias_ref,
                        o_ref, kn_ref):
    i = pl.program_id(1)
    kc = _head_rms(kc_ref[...], kg_ref[...], NKV)
    kp = _head_rms(kp_ref[...], kg_ref[...], NKV)
    kn_ref[...] = kc
    kk = jnp.concatenate([kp, kc], axis=0).astype(bf16)
    vv = jnp.concatenate([vp_ref[...], vc_ref[...]], axis=0).astype(bf16)
    q = q_ref[...]
    col = lax.broadcasted_iota(i32, (GQ * WINDOW, 2 * WINDOW), 1)
    valid = (i > 0) | (col >= WINDOW)
    for n in range(NKV):
        qn = jnp.concatenate(
            [_rms(q[:, (n * GQ + g) * HD:(n * GQ + g + 1) * HD], qg_ref[...]) for g in range(GQ)], axis=0)
        s = _dot_nt(qn.astype(bf16), kk[:, n * HD:(n + 1) * HD]) * (HD ** -0.5)
        s = s + bias_ref[n * GQ:(n + 1) * GQ].reshape(GQ * WINDOW, 2 * WINDOW)
        s = jnp.where(valid, s, NEG)
        sink_col = jnp.concatenate(
            [jnp.full((WINDOW, 1), sink_ref[n * GQ + g], f32) for g in range(GQ)], axis=0)
        o = _sink_softmax_pv(s, sink_col, vv[:, n * HD:(n + 1) * HD])
        for g in range(GQ):
            hh = n * GQ + g
            o_ref[:, hh * HD:(hh + 1) * HD] = o[g * WINDOW:(g + 1) * WINDOW].astype(bf16)


def _attn_prompt(proj, sinks, q_norm_g, k_norm_g, bias):
    nb = SEQ // WINDOW
    kblk, vblk = OFF_AK // KVW, OFF_AV // KVW

    def cur(col):
        return lambda b, i: (b * nb + i, col)

    def prev(col):
        return lambda b, i: (b * nb + jnp.maximum(i - 1, 0), col)

    return pl.pallas_call(
        _attn_prompt_kernel,
        grid=(BATCH, nb),
        in_specs=[pl.BlockSpec(memory_space=pltpu.SMEM),
                  pl.BlockSpec((WINDOW, NH * HD), cur(OFF_AQ // (NH * HD))),
                  pl.BlockSpec((WINDOW, KVW), cur(kblk)),
                  pl.BlockSpec((WINDOW, KVW), prev(kblk)),
                  pl.BlockSpec((WINDOW, KVW), cur(vblk)),
                  pl.BlockSpec((WINDOW, KVW), prev(vblk)),
                  pl.BlockSpec((1, HD), lambda b, i: (0, 0)),
                  pl.BlockSpec((1, HD), lambda b, i: (0, 0)),
                  pl.BlockSpec((NH, WINDOW, 2 * WINDOW), lambda b, i: (0, 0, 0))],
        out_specs=[pl.BlockSpec((WINDOW, NH * HD), lambda b, i: (b * nb + i, 0)),
                   pl.BlockSpec((WINDOW, KVW), lambda b, i: (b * nb + i, 0))],
        out_shape=[jax.ShapeDtypeStruct((TP, NH * HD), bf16),
                   jax.ShapeDtypeStruct((TP, KVW), f32)],
        compiler_params=_cparams(("arbitrary", "arbitrary"), 32),
        name="attn_prompt",
    )(sinks, proj, proj, proj, proj, proj, q_norm_g.reshape(1, HD), k_norm_g.reshape(1, HD), bias)


def _attn_sample_kernel(sink_ref, q_ref, k_ref, v_ref, ck_ref, cv_ref, qg_ref, kg_ref, bias_ref,
                        o_ref, kw_ref, vw_ref):
    pad_rows = 2 * WINDOW - WINDOW - SPAD
    for bi in range(SBB):
        kn = _head_rms(k_ref[:, bi, :], kg_ref[...], NKV)
        vn = v_ref[:, bi, :]
        ck = ck_ref[0, bi]
        cv = cv_ref[0, bi]
        kw_ref[bi, 0:WINDOW - DEC_SEQ, :] = ck[DEC_SEQ:, :]
        kw_ref[bi, WINDOW - DEC_SEQ:WINDOW, :] = kn[0:DEC_SEQ]
        vw_ref[bi, 0:WINDOW - DEC_SEQ, :] = cv[DEC_SEQ:, :]
        vw_ref[bi, WINDOW - DEC_SEQ:WINDOW, :] = vn[0:DEC_SEQ]
        zpad = jnp.zeros((pad_rows, KVW), f32)
        kk = jnp.concatenate([ck, kn, zpad], axis=0).astype(bf16)
        vv = jnp.concatenate([cv, vn, zpad], axis=0).astype(bf16)
        q = q_ref[:, bi, :]
        for n in range(NKV):
            qn = jnp.concatenate(
                [_rms(q[:, (n * GQ + g) * HD:(n * GQ + g + 1) * HD], qg_ref[...]) for g in range(GQ)], axis=0)
            s = _dot_nt(qn.astype(bf16), kk[:, n * HD:(n + 1) * HD]) * (HD ** -0.5)
            s = s + bias_ref[n * GQ:(n + 1) * GQ].reshape(GQ * SPAD, 2 * WINDOW)
            sink_col = jnp.concatenate(
                [jnp.full((SPAD, 1), sink_ref[n * GQ + g], f32) for g in range(GQ)], axis=0)
            o = _sink_softmax_pv(s, sink_col, vv[:, n * HD:(n + 1) * HD])
            for g in range(GQ):
                hh = n * GQ + g
                o_ref[:, bi, hh * HD:(hh + 1) * HD] = o[g * SPAD:(g + 1) * SPAD].astype(bf16)


def _attn_sample(l, qkv_pad, cache_k, cache_v, sinks, q_norm_g, k_norm_g, bias):
    kblk = NH * HD // KVW
    return pl.pallas_call(
        _attn_sample_kernel,
        grid=(DEC_BATCH // SBB,),
        in_specs=[pl.BlockSpec(memory_space=pltpu.SMEM),
                  pl.BlockSpec((SPAD, SBB, NH * HD), lambda s: (0, s, 0)),
                  pl.BlockSpec((SPAD, SBB, KVW), lambda s: (0, s, kblk)),
                  pl.BlockSpec((SPAD, SBB, KVW), lambda s: (0, s, kblk + 1)),
                  pl.BlockSpec((1, SBB, WINDOW, KVW), lambda s: (l, s, 0, 0)),
                  pl.BlockSpec((1, SBB, WINDOW, KVW), lambda s: (l, s, 0, 0)),
                  pl.BlockSpec((1, HD), lambda s: (0, 0)),
                  pl.BlockSpec((1, HD), lambda s: (0, 0)),
                  pl.BlockSpec((NH, SPAD, 2 * WINDOW), lambda s: (0, 0, 0))],
        out_specs=[pl.BlockSpec((SPAD, SBB, NH * HD), lambda s: (0, s, 0)),
                   pl.BlockSpec((SBB, WINDOW, KVW), lambda s: (s, 0, 0)),
                   pl.BlockSpec((SBB, WINDOW, KVW), lambda s: (s, 0, 0))],
        out_shape=[jax.ShapeDtypeStruct((SPAD, DEC_BATCH, NH * HD), bf16),
                   jax.ShapeDtypeStruct((DEC_BATCH, WINDOW, KVW), f32),
                   jax.ShapeDtypeStruct((DEC_BATCH, WINDOW, KVW), f32)],
        compiler_params=_cparams(("arbitrary",), 32),
        name="attn_sample",
    )(sinks, qkv_pad, qkv_pad, qkv_pad, cache_k, cache_v, q_norm_g.reshape(1, HD), k_norm_g.reshape(1, HD), bias)


def _merge_kernel(oh_ref, oa_ref, ga_ref, gb_ref, x_ref, g1p_ref, g1s_ref, wa_ref, wb_ref, wo_ref,
                  o_ref, acc_ref):
    i, j = pl.program_id(0), pl.program_id(1)
    merged = (jax.nn.sigmoid(ga_ref[...]) * _dot(oh_ref[...], wa_ref[0])
              + jax.nn.sigmoid(gb_ref[...]) * _dot(oa_ref[...], wb_ref[0]))
    part = _dot(merged.astype(bf16), wo_ref[0])

    @pl.when(j == 0)
    def _():
        acc_ref[...] = part

    @pl.when(j > 0)
    def _():
        acc_ref[...] += part

    @pl.when(j == pl.num_programs(1) - 1)
    def _():
        g1 = _mod_rows(i >= NPT, g1p_ref, g1s_ref, TM)
        o_ref[...] = x_ref[...] + g1 * acc_ref[...]


def _merge(l, o_h, o_a, proj, x, mod3, mod2, wa_bf, wb_bf, wo_bf):
    g1p, g1s = _mod_specs(2, TPB, NPT)
    return pl.pallas_call(
        _merge_kernel,
        grid=(NT, D // TN_MG),
        in_specs=[pl.BlockSpec((TM, KW), lambda i, j: (i, 0)),
                  pl.BlockSpec((TM, NH * HD), lambda i, j: (i, 0)),
                  pl.BlockSpec((TM, TN_MG), lambda i, j: (i, OFF_GA // TN_MG + j)),
                  pl.BlockSpec((TM, TN_MG), lambda i, j: (i, OFF_GB // TN_MG + j)),
                  pl.BlockSpec((TM, D), lambda i, j: (i, 0)),
                  g1p, g1s,
                  pl.BlockSpec((1, KW, TN_MG), lambda i, j: (l, 0, j)),
                  pl.BlockSpec((1, NH * HD, TN_MG), lambda i, j: (l, 0, j)),
                  pl.BlockSpec((1, TN_MG, D), lambda i, j: (l, j, 0))],
        out_specs=pl.BlockSpec((TM, D), lambda i, j: (i, 0)),
        out_shape=jax.ShapeDtypeStruct((T, D), f32),
        scratch_shapes=[pltpu.VMEM((TM, D), f32)],
        compiler_params=_cparams(("arbitrary", "arbitrary"), 48),
        name="merge",
    )(o_h, o_a, proj, proj, x, mod3, mod2, wa_bf, wb_bf, wo_bf)


def _first_index(hit, idx, big, axis):
    return jnp.min(jnp.where(hit, idx, big), axis=axis, keepdims=True)


def _pack_bf16_pairs(h):
    lo = lax.bitcast_convert_type(h[:, :D // 2].astype(bf16).astype(f32), u32)
    hi = lax.bitcast_convert_type(h[:, D // 2:].astype(bf16).astype(f32), u32)
    return (hi & jnp.uint32(0xFFFF0000)) | (lo >> 16)


def _unpack_bf16_pairs(p):
    lo = lax.bitcast_convert_type(p << 16, f32).astype(bf16)
    hi = lax.bitcast_convert_type(p & jnp.uint32(0xFFFF0000), f32).astype(bf16)
    return lo, hi


def _router_kernel(x_ref, g_ref, scp_ref, scs_ref, shp_ref, shs_ref, wr_ref, rb_ref,
                   hp_ref, ids_ref, wts_ref, rank_ref, cnt_ref):
    i = pl.program_id(0)
    y = _rms(x_ref[...], g_ref[...])
    sc = _mod_rows(i >= NPT, scp_ref, scs_ref, TM)
    sh = _mod_rows(i >= NPT, shp_ref, shs_ref, TM)
    h = y * (1.0 + sc) + sh
    hp_ref[...] = _pack_bf16_pairs(h).reshape(TM, ROW_TILES_X, LANES)
    wr = wr_ref[0]
    w1 = wr.astype(bf16)
    w2 = (wr - w1.astype(f32)).astype(bf16)
    h1 = h.astype(bf16)
    h2 = (h - h1.astype(f32)).astype(bf16)
    logits = _dot_nt(w1, h1) + (_dot_nt(w1, h2) + _dot_nt(w2, h1))
    scores = jax.nn.sigmoid(logits)
    biased = scores + rb_ref[...]
    b3 = biased.reshape(NG, EPG, TM)
    e_in = lax.broadcasted_iota(i32, (NG, EPG, TM), 1).astype(f32)
    m1 = jnp.max(b3, axis=1, keepdims=True)
    first = _first_index(b3 == m1, e_in, float(EPG), 1)
    m2 = jnp.max(jnp.where(e_in == first, -jnp.inf, b3), axis=1, keepdims=True)
    gs = (m1 + m2).reshape(NG, TM)
    g_idx = lax.broadcasted_iota(i32, (NG, TM), 0).astype(f32)
    gsel = jnp.zeros((NG, TM), f32)
    for _ in range(TOPKG):
        m = jnp.max(gs, axis=0, keepdims=True)
        pick = g_idx == _first_index(gs == m, g_idx, float(NG), 0)
        gsel = jnp.where(pick, 1.0, gsel)
        gs = jnp.where(pick, -jnp.inf, gs)
    emask = jnp.broadcast_to(gsel.reshape(NG, 1, TM), (NG, EPG, TM)).reshape(NE, TM)
    cur = jnp.where(emask > 0.5, biased, NEG)
    e_idx = lax.broadcasted_iota(i32, (NE, TM), 0).astype(f32)
    ids, ws, picks = [], [], []
    for _ in range(TOPK):
        m = jnp.max(cur, axis=0, keepdims=True)
        first = _first_index(cur == m, e_idx, float(NE), 0)
        pick = e_idx == first
        ids.append(first)
        picks.append(pick)
        ws.append(jnp.sum(jnp.where(pick, scores, 0.0), axis=0, keepdims=True))
        cur = jnp.where(pick, -jnp.inf, cur)
    w = jnp.concatenate(ws, axis=0)
    ids_ref[...] = jnp.concatenate(ids, axis=0).astype(i32)
    wts_ref[...] = w / jnp.sum(w, axis=0, keepdims=True) * ROUTED_SCALE

    @pl.when(i == 0)
    def _():
        cnt_ref[...] = jnp.zeros_like(cnt_ref)

    onehot = jnp.zeros((NE, TM), f32)
    for pick in picks:
        onehot = jnp.where(pick, 1.0, onehot)
    before = (lax.broadcasted_iota(i32, (TM, TM), 0) < lax.broadcasted_iota(i32, (TM, TM), 1)).astype(bf16)
    base = cnt_ref[:, 0:1] + _dot(onehot.astype(bf16), before)
    rank_ref[...] = jnp.concatenate(
        [jnp.sum(jnp.where(pick, base, 0.0), axis=0, keepdims=True) for pick in picks], axis=0).astype(i32)
    cnt_ref[...] += jnp.sum(onehot, axis=1, keepdims=True)


def _router(l, x1, norm_g, mod3, mod2, w_router_t, router_bias):
    scp, scs = _mod_specs(4, TPB, NPT)
    shp, shs = _mod_specs(3, TPB, NPT)
    return pl.pallas_call(
        _router_kernel,
        grid=(NT,),
        in_specs=[pl.BlockSpec((TM, D), lambda i: (i, 0)),
                  pl.BlockSpec((1, D), lambda i: (0, 0)),
                  scp, scs, shp, shs,
                  pl.BlockSpec((1, NE, D), lambda i: (l, 0, 0)),
                  pl.BlockSpec((NE, 1), lambda i: (0, 0))],
        out_specs=[pl.BlockSpec((TM, ROW_TILES_X, LANES), lambda i: (i, 0, 0)),
                   pl.BlockSpec((TOPK, TM), lambda i: (0, i)),
                   pl.BlockSpec((TOPK, TM), lambda i: (0, i)),
                   pl.BlockSpec((TOPK, TM), lambda i: (0, i)),
                   pl.BlockSpec((NE, LANES), lambda i: (0, 0))],
        out_shape=[jax.ShapeDtypeStruct((T, ROW_TILES_X, LANES), u32),
                   jax.ShapeDtypeStruct((TOPK, T), i32),
                   jax.ShapeDtypeStruct((TOPK, T), f32),
                   jax.ShapeDtypeStruct((TOPK, T), i32),
                   jax.ShapeDtypeStruct((NE, LANES), f32)],
        compiler_params=_cparams(("arbitrary",), 48),
        name="router",
    )(x1, norm_g.reshape(1, D), mod3, mod2, mod3, mod2, w_router_t, router_bias.reshape(NE, 1))


def _block_tables(cnt):
    counts = cnt[:, 0].astype(i32)
    padded = (counts + BLK - 1) // BLK * BLK
    pend = jnp.cumsum(padded)
    pstart = pend - padded
    blk_start = jnp.arange(NBLK, dtype=i32) * BLK
    block_e = jnp.minimum(jnp.sum(blk_start[:, None] >= pend[None, :], axis=1), NE - 1).astype(i32)
    n_used = (pend[-1] // BLK).astype(i32).reshape(1)
    return block_e, n_used, pstart.astype(i32), (pstart + counts).astype(i32), (padded - counts).astype(i32)


def _pos_kernel(ps_ref, ids_ref, rank_ref, pos_ref):
    ids = ids_ref[...]
    acc = rank_ref[...]
    for e in range(NE):
        acc = acc + jnp.where(ids == e, ps_ref[e], 0)
    pos_ref[...] = acc


def _slot_positions(pstart, ids_t, rank_t):
    return pl.pallas_call(
        _pos_kernel,
        grid=(1,),
        in_specs=[pl.BlockSpec(memory_space=pltpu.SMEM),
                  pl.BlockSpec((TOPK, T), lambda i: (0, 0)),
                  pl.BlockSpec((TOPK, T), lambda i: (0, 0))],
        out_specs=pl.BlockSpec((TOPK, T), lambda i: (0, 0)),
        out_shape=jax.ShapeDtypeStruct((TOPK, T), i32),
        compiler_params=_cparams(("arbitrary",), 16),
        name="slot_positions",
    )(pstart, ids_t, rank_t)


def _dispatch_kernel(pad0_ref, padn_ref, nu_ref, pos_ref, hp_hbm, xs_hbm, zero_ref, sem):
    i = pl.program_id(0)

    def tok_copy(k, j, u):
        r = j * SUBLANES + u
        return pltpu.make_async_copy(hp_hbm.at[i * TMF + r], xs_hbm.at[pos_ref[0, 0, k * TMF + r]], sem)

    def pad_copy(e, r):
        return pltpu.make_async_copy(zero_ref.at[0], xs_hbm.at[pad0_ref[e] + r], sem)

    def tail_copy(b):
        return pltpu.make_async_copy(zero_ref, xs_hbm.at[pl.ds(b * BLK, BLK)], sem)

    def for_tokens(fn):
        for k in range(TOPK):
            def body(j, c):
                for u in range(SUBLANES):
                    fn(tok_copy(k, j, u), u)
                return c
            lax.fori_loop(0, TMF // SUBLANES, body, 0)

    def for_pads(fn):
        def per_expert(e, c):
            def body(r, c2):
                fn(pad_copy(e, r), 0)
                return c2
            return lax.fori_loop(0, padn_ref[e], body, c)
        lax.fori_loop(0, NE, per_expert, 0)

        def per_block(b, c):
            fn(tail_copy(b), 0)
            return c
        lax.fori_loop(nu_ref[0], NBLK, per_block, 0)

    @pl.when(i == 0)
    def _():
        zero_ref[...] = jnp.zeros_like(zero_ref)
        for_pads(lambda cp, u: cp.start(priority=u % 2))

    for_tokens(lambda cp, u: cp.start(priority=u % 2))
    for_tokens(lambda cp, u: cp.wait())

    @pl.when(i == 0)
    def _():
        for_pads(lambda cp, u: cp.wait())


def _dispatch(pad_start, pad_cnt, n_used, pos_tiles, hp):
    grid_spec = pltpu.PrefetchScalarGridSpec(
        num_scalar_prefetch=3,
        grid=(T // TMF,),
        in_specs=[pl.BlockSpec((1, 1, TOPK * TMF), lambda i, p0, pn, nu: (i, 0, 0), memory_space=pltpu.SMEM),
                  pl.BlockSpec(memory_space=pl.ANY)],
        out_specs=pl.BlockSpec(memory_space=pl.ANY),
        scratch_shapes=[pltpu.VMEM((BLK, ROW_TILES_X, LANES), u32), pltpu.SemaphoreType.DMA(())],
    )
    return pl.pallas_call(
        _dispatch_kernel,
        grid_spec=grid_spec,
        out_shape=jax.ShapeDtypeStruct((NROW, ROW_TILES_X, LANES), u32),
        compiler_params=_cparams(("arbitrary",), 16),
        name="dispatch",
    )(pad_start, pad_cnt, n_used, pos_tiles, hp)


def _expert_kernel(be_ref, nu_ref, x_ref, wg_ref, wu_ref, wd_ref, y_ref, wgb, wub, wdb):
    i = pl.program_id(0)

    @pl.when(i >= nu_ref[0])
    def _():
        y_ref[...] = jnp.zeros_like(y_ref)

    @pl.when(i < nu_ref[0])
    def _():
        @pl.when((i == 0) | (be_ref[i] != be_ref[jnp.maximum(i - 1, 0)]))
        def _():
            wgb[...] = wg_ref[0, 0].astype(bf16)
            wub[...] = wu_ref[0, 0].astype(bf16)
            wdb[...] = wd_ref[0, 0].astype(bf16)

        lo, hi = _unpack_bf16_pairs(x_ref[...].reshape(BLK, D // 2))
        half = D // 2
        g = _dot(lo, wgb[0:half, :]) + _dot(hi, wgb[half:D, :])
        u = _dot(lo, wub[0:half, :]) + _dot(hi, wub[half:D, :])
        y = _dot((_silu(g) * u).astype(bf16), wdb[...])
        y_ref[...] = _pack_bf16_pairs(y).reshape(BLK, ROW_TILES_X, LANES)


def _experts(l, xs, block_e, n_used, w_gate, w_up, w_down):
    def blk(i, be, nu):
        return jnp.minimum(i, nu[0] - 1)

    grid_spec = pltpu.PrefetchScalarGridSpec(
        num_scalar_prefetch=2,
        grid=(NBLK,),
        in_specs=[pl.BlockSpec((BLK, ROW_TILES_X, LANES), lambda i, be, nu: (blk(i, be, nu), 0, 0)),
                  pl.BlockSpec((1, 1, D, DE), lambda i, be, nu: (l, be[blk(i, be, nu)], 0, 0)),
                  pl.BlockSpec((1, 1, D, DE), lambda i, be, nu: (l, be[blk(i, be, nu)], 0, 0)),
                  pl.BlockSpec((1, 1, DE, D), lambda i, be, nu: (l, be[blk(i, be, nu)], 0, 0))],
        out_specs=pl.BlockSpec((BLK, ROW_TILES_X, LANES), lambda i, be, nu: (i, 0, 0)),
        scratch_shapes=[pltpu.VMEM((D, DE), bf16), pltpu.VMEM((D, DE), bf16), pltpu.VMEM((DE, D), bf16)],
    )
    return pl.pallas_call(
        _expert_kernel,
        grid_spec=grid_spec,
        out_shape=jax.ShapeDtypeStruct((NROW, ROW_TILES_X, LANES), u32),
        compiler_params=_cparams(("arbitrary",), 56),
        name="experts",
    )(block_e, n_used, xs, w_gate, w_up, w_down)


def _final_kernel(pos_ref, posn_ref, x_ref, hp_ref, w_ref, g2p_ref, g2s_ref, sg_ref, su_ref, sd_ref,
                  y_hbm, o_ref, ybuf, sem):
    i = pl.program_id(0)
    n = pl.num_programs(0)
    slot = i % 2

    def row_copy(pref, k, j, u, s):
        r = j * SUBLANES + u
        return pltpu.make_async_copy(y_hbm.at[pref[0, 0, k * TMF + r]], ybuf.at[s, k, r], sem.at[s])

    def for_rows(pref, s, fn):
        for k in range(TOPK):
            def body(j, c):
                for u in range(SUBLANES):
                    fn(row_copy(pref, k, j, u, s), u)
                return c
            lax.fori_loop(0, TMF // SUBLANES, body, 0)

    @pl.when(i == 0)
    def _():
        for_rows(pos_ref, 0, lambda cp, u: cp.start(priority=u % 2))

    @pl.when(i + 1 < n)
    def _():
        for_rows(posn_ref, 1 - slot, lambda cp, u: cp.start(priority=u % 2))

    lo, hi = _unpack_bf16_pairs(hp_ref[...].reshape(TMF, D // 2))
    hb = jnp.concatenate([lo, hi], axis=1)
    shared = _dot((_silu(_dot(hb, sg_ref[0])) * _dot(hb, su_ref[0])).astype(bf16), sd_ref[0])

    for_rows(pos_ref, slot, lambda cp, u: cp.wait())

    w = w_ref[...]
    r_lo = jnp.zeros((TMF, D // 2), f32)
    r_hi = jnp.zeros((TMF, D // 2), f32)
    for k in range(TOPK):
        p = ybuf[slot, k].reshape(TMF, D // 2)
        r_lo = r_lo + w[:, k:k + 1] * lax.bitcast_convert_type(p << 16, f32)
        r_hi = r_hi + w[:, k:k + 1] * lax.bitcast_convert_type(p & jnp.uint32(0xFFFF0000), f32)
    routed = jnp.concatenate([r_lo, r_hi], axis=1)
    g2 = _mod_rows(i >= TP // TMF, g2p_ref, g2s_ref, TMF)
    o_ref[...] = x_ref[...] + g2 * (shared + routed)


def _final(l, x1, hp, wts, pos_tiles, mod3, mod2, sg_bf, su_bf, sd_bf, yb):
    g2p, g2s = _mod_specs(5, SEQ // TMF, TP // TMF)
    ntile = T // TMF
    return pl.pallas_call(
        _final_kernel,
        grid=(ntile,),
        in_specs=[pl.BlockSpec((1, 1, TOPK * TMF), lambda i: (i, 0, 0), memory_space=pltpu.SMEM),
                  pl.BlockSpec((1, 1, TOPK * TMF), lambda i: (jnp.minimum(i + 1, ntile - 1), 0, 0),
                               memory_space=pltpu.SMEM),
                  pl.BlockSpec((TMF, D), lambda i: (i, 0)),
                  pl.BlockSpec((TMF, ROW_TILES_X, LANES), lambda i: (i, 0, 0)),
                  pl.BlockSpec((TMF, TOPK), lambda i: (i, 0)),
                  g2p, g2s,
                  pl.BlockSpec((1, D, DE), lambda i: (l, 0, 0)),
                  pl.BlockSpec((1, D, DE), lambda i: (l, 0, 0)),
                  pl.BlockSpec((1, DE, D), lambda i: (l, 0, 0)),
                  pl.BlockSpec(memory_space=pl.ANY)],
        out_specs=pl.BlockSpec((TMF, D), lambda i: (i, 0)),
        out_shape=jax.ShapeDtypeStruct((T, D), f32),
        scratch_shapes=[pltpu.VMEM((2, TOPK, TMF, ROW_TILES_X, LANES), u32),
                        pltpu.SemaphoreType.DMA((2,))],
        compiler_params=_cparams(("arbitrary",), 56),
        name="moe_combine",
    )(pos_tiles, pos_tiles, x1, hp, wts, mod3, mod2, sg_bf, su_bf, sd_bf, yb)


def _layer(l, x, mod, lb, state, cache_k, cache_v, bias_p, bias_s, norm1_g, norm2_g, w_in_bf, onorm_g,
           q_norm_g, k_norm_g, sinks, wa_bf, wb_bf, wo_bf, w_router_t, router_bias,
           w_eg, w_eu, w_ed, sg_bf, su_bf, sd_bf):
    mod3 = mod.reshape(mod.shape[0], 1, 6 * D)
    proj = _in_proj(l, x, norm1_g, mod3, mod, w_in_bf)

    oh_p, s_p = _hgrn_prompt(proj, lb, onorm_g)
    oh_s, s_s = _hgrn_sample(l, proj, lb, onorm_g, state)

    oa_p, kn_p = _attn_prompt(proj, sinks, q_norm_g, k_norm_g, bias_p)
    qkv_pad = jnp.pad(proj[TP:, OFF_AQ:OFF_GA].reshape(DEC_SEQ, DEC_BATCH, OFF_GA - OFF_AQ),
                      ((0, SPAD - DEC_SEQ), (0, 0), (0, 0)))
    oa_s, kw_s, vw_s = _attn_sample(l, qkv_pad, cache_k, cache_v, sinks, q_norm_g, k_norm_g, bias_s)

    o_h = jnp.concatenate([oh_p, oh_s], axis=0)
    o_a = jnp.concatenate([oa_p, oa_s[:DEC_SEQ].reshape(TS, NH * HD)], axis=0)
    x1 = _merge(l, o_h, o_a, proj, x, mod3, mod, wa_bf, wb_bf, wo_bf)

    hp, ids_t, wts_t, rank_t, cnt = _router(l, x1, norm2_g, mod3, mod, w_router_t, router_bias)
    block_e, n_used, pstart, pad_start, pad_cnt = _block_tables(cnt)
    pos = _slot_positions(pstart, ids_t, rank_t)
    pos_tiles = pos.reshape(TOPK, T // TMF, TMF).transpose(1, 0, 2).reshape(T // TMF, 1, TOPK * TMF)
    xs = _dispatch(pad_start, pad_cnt, n_used, pos_tiles, hp)
    yb = _experts(l, xs, block_e, n_used, w_eg, w_eu, w_ed)
    x2 = _final(l, x1, hp, wts_t.T, pos_tiles, mod3, mod, sg_bf, su_bf, sd_bf, yb)

    k_p = kn_p.reshape(BATCH, SEQ, NKV, HD)[:, SEQ - WINDOW:]
    v_p = proj[:TP, OFF_AV:OFF_GA].reshape(BATCH, SEQ, NKV, HD)[:, SEQ - WINDOW:]
    return (x2, s_p, s_s, k_p, v_p, kw_s.reshape(DEC_BATCH, WINDOW, NKV, HD),
            vw_s.reshape(DEC_BATCH, WINDOW, NKV, HD))


def kernel(x_prompt, x_sample, state_hgrn, cache_k_win, cache_v_win, c_prompt, c_sample, norm1_g, norm2_g,
           w_ada, b_ada, w_in, hgrn_lb_logits, hgrn_onorm_g, q_norm_g, k_norm_g, attn_sinks, rel_bias,
           w_branch_a, w_branch_b, w_out, w_router, router_bias, w_exp_gate, w_exp_up, w_exp_down,
           w_sh_gate, w_sh_up, w_sh_down):
    p_lb = jax.nn.softmax(hgrn_lb_logits.astype(f32), axis=0)
    lower_bounds = jnp.clip(jnp.cumsum(p_lb, axis=0) - p_lb[0], 0.0, 1.0 - 1e-6)

    c_all = jnp.concatenate([c_sample, c_prompt, jnp.zeros((SUBLANES - BATCH, D), f32)], axis=0)
    mod = _ada_mod(c_all, w_ada, b_ada)
    bias_p = _bias_table(rel_bias, _prompt_buckets())
    bias_s = _bias_table(rel_bias, _sample_buckets())

    x = jnp.concatenate([x_prompt.reshape(TP, D), x_sample.transpose(1, 0, 2).reshape(TS, D)], axis=0)
    cache_k = cache_k_win.reshape(DEPTH, DEC_BATCH, WINDOW, KVW)
    cache_v = cache_v_win.reshape(DEPTH, DEC_BATCH, WINDOW, KVW)
    w_in_bf, wa_bf, wb_bf, wo_bf = (w.astype(bf16) for w in (w_in, w_branch_a, w_branch_b, w_out))
    sg_bf, su_bf, sd_bf = (w.astype(bf16) for w in (w_sh_gate, w_sh_up, w_sh_down))
    w_router_t = w_router.transpose(0, 2, 1)
    sp_l, ss_l, kp_l, vp_l, ks_l, vs_l = [], [], [], [], [], []
    for l in range(DEPTH):
        x, s_p, s_s, k_p, v_p, k_s, v_s = _layer(
            l, x, mod[l], lower_bounds[l], state_hgrn, cache_k, cache_v, bias_p, bias_s,
            norm1_g[l], norm2_g[l], w_in_bf, hgrn_onorm_g[l], q_norm_g[l], k_norm_g[l], attn_sinks[l],
            wa_bf, wb_bf, wo_bf, w_router_t, router_bias[l],
            w_exp_gate, w_exp_up, w_exp_down, sg_bf, su_bf, sd_bf)
        sp_l.append(s_p); ss_l.append(s_s); kp_l.append(k_p); vp_l.append(v_p); ks_l.append(k_s); vs_l.append(v_s)

    y_prompt = x[:TP].reshape(BATCH, SEQ, D)
    y_sample = x[TP:].reshape(DEC_SEQ, DEC_BATCH, D).transpose(1, 0, 2)
    return (y_prompt, y_sample, jnp.stack(sp_l), jnp.stack(ss_l), jnp.stack(kp_l), jnp.stack(vp_l),
            jnp.stack(ks_l), jnp.stack(vs_l))
```

```python
import functools
import math

import numpy as np
import jax
import jax.numpy as jnp
from jax import lax
from jax.experimental import pallas as pl
from jax.experimental.pallas import tpu as pltpu

f32 = jnp.float32
bf16 = jnp.bfloat16
i32 = jnp.int32
u32 = jnp.uint32

D = 2048
BATCH, SEQ = 2, 4096
DEPTH = 2
DEC_BATCH, DEC_SEQ = 128, 4
HH, DK, DV = 8, 128, 128
KW = HH * DK
NH, NKV, HD = 16, 4, 64
GQ = NH // NKV
KVW = NKV * HD
WINDOW = 128
NUM_BUCKETS, MAX_DISTANCE = 32, 128
NE, NG, TOPKG, TOPK, DE = 64, 8, 4, 8, 512
EPG = NE // NG
ROUTED_SCALE = 2.5
IN_WIDTH = 2 * KW + 2 * KW + NH * HD + 2 * KVW + 2 * D
EPS = 1e-6
NEG = -1e30
TINY = 1e-30

OFF_HQ, OFF_HF, OFF_HV, OFF_HG = 0, KW, 2 * KW, 3 * KW
OFF_AQ = 4 * KW
OFF_AK = OFF_AQ + NH * HD
OFF_AV = OFF_AK + KVW
OFF_GA = OFF_AV + KVW
OFF_GB = OFF_GA + D

TP = BATCH * SEQ
TS = DEC_BATCH * DEC_SEQ
T = TP + TS
TM = 512
NPT = TP // TM
NT = T // TM
TPB = SEQ // TM
TN_IN = 2432
TN_MG = 512
TMF = 128
HC = 256
HGRN_WIDTHS = (128, 64, 32, 16, 8, 4)
SBB = 8
SPAD = 8
BLK = 256
NA = T * TOPK
NBLK = NA // BLK + NE
NROW = NBLK * BLK

V7X_VMEM_BYTES = 64 * 1024 * 1024
SUBLANES, LANES = 8, 128
ROW_TILES_X = D // 2 // LANES


def _cparams(sem, vmem_mib):
    assert vmem_mib * 1024 * 1024 < V7X_VMEM_BYTES
    return pltpu.CompilerParams(dimension_semantics=sem, vmem_limit_bytes=vmem_mib * 1024 * 1024)


def _silu(x):
    return x * jax.nn.sigmoid(x)


def _rms(x, g):
    return x * lax.rsqrt(jnp.mean(x * x, axis=-1, keepdims=True) + EPS) * g


def _dot(a, b):
    return jnp.dot(a, b, preferred_element_type=f32)


def _dot_nt(a, b):
    return lax.dot_general(a, b, (((1,), (1,)), ((), ())), preferred_element_type=f32)


def _dot_tn(a, b):
    return lax.dot_general(a, b, (((0,), (0,)), ((), ())), preferred_element_type=f32)


def _split3(x):
    a = x.astype(bf16)
    r = x - a.astype(f32)
    b = r.astype(bf16)
    c = (r - b.astype(f32)).astype(bf16)
    return a, b, c


def _mod_rows(is_sample, p_ref, s_ref, rows):
    p = p_ref[0]
    s = s_ref[...]
    if rows != DEC_BATCH:
        s = jnp.broadcast_to(s[None], (rows // DEC_BATCH, DEC_BATCH, D)).reshape(rows, D)
    return jnp.where(is_sample, s, p)


def _mod_specs(chunk, tiles_per_seq, n_prompt_tiles, extra_args=0):
    def pmap(i, *_):
        return (DEC_BATCH + jnp.minimum(i // tiles_per_seq, BATCH - 1), 0, chunk)

    def smap(i, *_):
        return (0, chunk)

    return (pl.BlockSpec((1, 1, D), pmap), pl.BlockSpec((DEC_BATCH, D), smap))


def _ada_kernel(c_ref, w_ref, b_ref, o_ref):
    c = c_ref[...]
    a = _silu(c).astype(bf16)
    o_ref[0] = _dot(a, w_ref[0].astype(bf16)) + b_ref[0]


def _ada_mod(c_all, w_ada, b_ada):
    rows = c_all.shape[0]
    tn = 1024
    return pl.pallas_call(
        _ada_kernel,
        grid=(DEPTH, 6 * D // tn),
        in_specs=[pl.BlockSpec((rows, D), lambda l, j: (0, 0)),
                  pl.BlockSpec((1, D, tn), lambda l, j: (l, 0, j)),
                  pl.BlockSpec((1, 1, tn), lambda l, j: (l, 0, j))],
        out_specs=pl.BlockSpec((1, rows, tn), lambda l, j: (l, 0, j)),
        out_shape=jax.ShapeDtypeStruct((DEPTH, rows, 6 * D), f32),
        compiler_params=_cparams(("arbitrary", "arbitrary"), 40),
        name="ada_mod",
    )(c_all, w_ada, b_ada.reshape(DEPTH, 1, 6 * D))


def _bucket_np(dist):
    n = np.maximum(dist, 0)
    exact = NUM_BUCKETS // 2
    nf = np.maximum(n, 1).astype(np.float32)
    large = exact + (np.log(nf / np.float32(exact)) / np.float32(math.log(MAX_DISTANCE / exact))
                     * np.float32(NUM_BUCKETS - exact)).astype(np.int32)
    return np.where(n < exact, n, np.clip(large, exact, NUM_BUCKETS - 1)).astype(np.int32)


def _prompt_buckets():
    j = np.arange(2 * WINDOW)
    dist = np.arange(WINDOW)[:, None] + WINDOW - j[None, :]
    band = (dist >= 0) & (dist < WINDOW)
    return np.where(band, _bucket_np(dist), -1).astype(np.int32)


def _sample_buckets():
    l = np.arange(SPAD)[:, None]
    j = np.arange(2 * WINDOW)[None, :]
    dist = WINDOW + l - j
    ok = (dist >= 0) & (dist < WINDOW) & (l < DEC_SEQ) & (j < WINDOW + DEC_SEQ)
    return np.where(ok, _bucket_np(dist), -1).astype(np.int32)


def _bias_kernel(tab_ref, bk_ref, o_ref):
    h = pl.program_id(0)
    bk = bk_ref[...]
    acc = jnp.zeros(bk.shape, f32)
    for i in range(NUM_BUCKETS):
        acc = jnp.where(bk == i, tab_ref[i, h], acc)
    o_ref[0] = jnp.where(bk < 0, NEG, acc)


def _bias_table(rel_bias, buckets):
    r, c = buckets.shape
    return pl.pallas_call(
        _bias_kernel,
        grid=(NH,),
        in_specs=[pl.BlockSpec(memory_space=pltpu.SMEM),
                  pl.BlockSpec((r, c), lambda h: (0, 0))],
        out_specs=pl.BlockSpec((1, r, c), lambda h: (h, 0, 0)),
        out_shape=jax.ShapeDtypeStruct((NH, r, c), f32),
        compiler_params=_cparams(("arbitrary",), 16),
        name="bias_table",
    )(rel_bias, jnp.asarray(buckets))


def _inproj_kernel(x_ref, g_ref, scp_ref, scs_ref, shp_ref, shs_ref, w_ref, o_ref, h_ref):
    i = pl.program_id(0)

    @pl.when(pl.program_id(1) == 0)
    def _():
        y = _rms(x_ref[...], g_ref[...])
        sc = _mod_rows(i >= NPT, scp_ref, scs_ref, TM)
        sh = _mod_rows(i >= NPT, shp_ref, shs_ref, TM)
        h_ref[...] = (y * (1.0 + sc) + sh).astype(bf16)

    o_ref[...] = _dot(h_ref[...], w_ref[0])


def _in_proj(l, x, norm_g, mod3, mod2, w_in_bf):
    scp, scs = _mod_specs(1, TPB, NPT)
    shp, shs = _mod_specs(0, TPB, NPT)
    return pl.pallas_call(
        _inproj_kernel,
        grid=(NT, IN_WIDTH // TN_IN),
        in_specs=[pl.BlockSpec((TM, D), lambda i, j: (i, 0)),
                  pl.BlockSpec((1, D), lambda i, j: (0, 0)),
                  scp, scs, shp, shs,
                  pl.BlockSpec((1, D, TN_IN), lambda i, j: (l, 0, j))],
        out_specs=pl.BlockSpec((TM, TN_IN), lambda i, j: (i, j)),
        out_shape=jax.ShapeDtypeStruct((T, IN_WIDTH), f32),
        scratch_shapes=[pltpu.VMEM((TM, D), bf16)],
        compiler_params=_cparams(("arbitrary", "arbitrary"), 56),
        name="in_proj",
    )(x, norm_g.reshape(1, D), mod3, mod2, mod3, mod2, w_in_bf)


def _hgrn_gates(z, lb):
    f = lb + (1.0 - lb) * jax.nn.sigmoid(z)
    g = jnp.log(jnp.maximum(f, TINY))
    k = (1.0 - lb) * jax.nn.sigmoid(-z)
    return g, k


def _bcast_block_row(b, period, row):
    c, w = b.shape
    b3 = b.reshape(c // period, period, w)
    return jnp.broadcast_to(b3[:, row:row + 1, :], b3.shape).reshape(c, w)


def _hgrn_level_table():
    t = np.arange(HC)[:, None]
    s = np.arange(HC)[None, :]
    lev = np.full((HC, HC), -1, np.int32)
    for li, w in enumerate(HGRN_WIDTHS + (2,)):
        m = (t // (2 * w) == s // (2 * w)) & ((t // w) % 2 == 1) & ((s // w) % 2 == 0)
        lev[m] = li
    lev[(t // 2 == s // 2) & (s <= t)] = len(HGRN_WIDTHS) + 1
    return lev


def _hgrn_prompt_kernel(q_ref, f_ref, v_ref, og_ref, lb_ref, ng_ref, lev_ref, o_ref, sfin_ref, st_ref):
    c = pl.program_id(1)

    @pl.when(c == 0)
    def _():
        st_ref[...] = jnp.zeros_like(st_ref)

    lb = lb_ref[...]
    g, k = _hgrn_gates(f_ref[...], lb)
    qh = _silu(q_ref[...])
    vb = v_ref[...].astype(bf16)
    r = lax.broadcasted_iota(i32, (HC, HC), 0)
    s = lax.broadcasted_iota(i32, (HC, HC), 1)
    tri = (r >= s).astype(bf16)
    g1, g2, g3 = _split3(g)
    b = _dot(tri, g1) + _dot(tri, g2) + _dot(tri, g3)
    lev = lev_ref[...]

    qs, ks = [], []
    for w in HGRN_WIDTHS:
        m = _bcast_block_row(b, 2 * w, w - 1)
        qs.append((qh * jnp.exp(jnp.minimum(b - m, 0.0))).astype(bf16))
        ks.append((k * jnp.exp(jnp.minimum(m - b, 0.0))).astype(bf16))
    row = lax.broadcasted_iota(i32, (HC, 1), 0)
    g_prev = pltpu.roll(g, 1, 0)
    g_next = pltpu.roll(g, HC - 1, 0)
    r4 = row & 3
    qs.append((qh * jnp.exp(jnp.where(r4 == 2, g, jnp.where(r4 == 3, g + g_prev, 0.0)))).astype(bf16))
    ks.append((k * jnp.exp(jnp.where(r4 == 0, g_next, 0.0))).astype(bf16))
    odd = (row & 1) == 1
    qs.append((qh * jnp.exp(jnp.where(odd, g, 0.0))).astype(bf16))
    ks.append((k * jnp.exp(jnp.where(odd, -g, 0.0))).astype(bf16))
    b_end = b[HC - 1:HC, :]
    q_in = (qh * jnp.exp(b)).astype(bf16)
    k_out = (k * jnp.exp(b_end - b)).astype(bf16)
    e_end = jnp.exp(b_end)
    gate = _silu(og_ref[...])
    ng = ng_ref[...]

    for h in range(HH):
        sl = slice(h * DK, (h + 1) * DK)
        a = jnp.zeros((HC, HC), f32)
        for li in range(len(qs)):
            a = jnp.where(lev == li, _dot_nt(qs[li][:, sl], ks[li][:, sl]), a)
        st = st_ref[h]
        o = _dot(a.astype(bf16), vb[:, sl]) + _dot_nt(q_in[:, sl], st.astype(bf16))
        st_new = st * e_end[:, sl] + _dot_tn(vb[:, sl], k_out[:, sl])
        st_ref[h] = st_new
        o_ref[:, sl] = (_rms(o, ng) * gate[:, sl]).astype(bf16)

        @pl.when(c == pl.num_programs(1) - 1)
        def _():
            sfin_ref[0, h] = st_new.T


def _hgrn_prompt(proj, lb, onorm_g):
    nc = SEQ // HC

    def slab(k):
        return pl.BlockSpec((HC, KW), lambda b, c: (b * nc + c, k))

    return pl.pallas_call(
        _hgrn_prompt_kernel,
        grid=(BATCH, nc),
        in_specs=[slab(OFF_HQ // KW), slab(OFF_HF // KW), slab(OFF_HV // KW), slab(OFF_HG // KW),
                  pl.BlockSpec((1, KW), lambda b, c: (0, 0)),
                  pl.BlockSpec((1, DV), lambda b, c: (0, 0)),
                  pl.BlockSpec((HC, HC), lambda b, c: (0, 0))],
        out_specs=[pl.BlockSpec((HC, KW), lambda b, c: (b * nc + c, 0)),
                   pl.BlockSpec((1, HH, DK, DV), lambda b, c: (b, 0, 0, 0))],
        out_shape=[jax.ShapeDtypeStruct((TP, KW), bf16),
                   jax.ShapeDtypeStruct((BATCH, HH, DK, DV), f32)],
        scratch_shapes=[pltpu.VMEM((HH, DV, DK), f32)],
        compiler_params=_cparams(("arbitrary", "arbitrary"), 48),
        name="hgrn_prompt",
    )(proj, proj, proj, proj, lb.reshape(1, KW), onorm_g.reshape(1, DV), jnp.asarray(_hgrn_level_table()))


def _hgrn_sample_kernel(q_ref, f_ref, v_ref, og_ref, lb_ref, ng_ref, s0_ref, o_ref, snew_ref,
                        qin_ref, kout_ref, eend_ref, oacc_ref, mt_ref, q16_ref, ostage_ref):
    step = pl.program_id(0)
    L, B = DEC_SEQ, DEC_BATCH

    @pl.when(step == 0)
    def _():
        lb = lb_ref[...]
        g, k = _hgrn_gates(f_ref[...], lb)
        qh = _silu(q_ref[...])
        v = v_ref[...]
        bs = []
        acc = None
        for t in range(L):
            gt = g[t * B:(t + 1) * B]
            acc = gt if acc is None else acc + gt
            bs.append(acc)
        b_end = bs[-1]
        eend_ref[...] = jnp.exp(b_end)
        for t in range(L):
            rt = slice(t * B, (t + 1) * B)
            qin_ref[rt, :] = qh[rt] * jnp.exp(bs[t])
            kout_ref[rt, :] = k[rt] * jnp.exp(b_end - bs[t])
            for h in range(HH):
                sl = slice(h * DK, (h + 1) * DK)
                o = jnp.zeros((B, DV), f32)
                for s_ in range(t + 1):
                    rs = slice(s_ * B, (s_ + 1) * B)
                    w = jnp.sum(qh[rt, sl] * k[rs, sl] * jnp.exp(bs[t][:, sl] - bs[s_][:, sl]),
                                axis=-1, keepdims=True)
                    o = o + w * v[rs, sl]
                oacc_ref[rt, sl] = o
        mt_ref[...] = jnp.zeros_like(mt_ref)
        q16_ref[...] = jnp.zeros_like(q16_ref)

    rows = [pl.ds(pl.multiple_of(t * B + step * SBB, SBB), SBB) for t in range(L)]
    q_t = [qin_ref[rows[t], :] for t in range(L)]
    k_t = [kout_ref[rows[t], :] for t in range(L)]
    v_t = [v_ref[rows[t], :] for t in range(L)]
    e_t = eend_ref[pl.ds(pl.multiple_of(step * SBB, SBB), SBB), :]
    for bi in range(SBB):
        for h in range(HH):
            sl = slice(h * DK, (h + 1) * DK)
            s0 = s0_ref[0, bi, h]
            for t in range(L):
                q16_ref[t:t + 1, :] = q_t[t][bi:bi + 1, sl]
            oi = _dot(q16_ref[...].astype(bf16), s0.astype(bf16))
            for t in range(L):
                ostage_ref[t, bi:bi + 1, sl] = oi[t:t + 1]
            mt_ref[0:1, :] = e_t[bi:bi + 1, sl]
            for t in range(L):
                mt_ref[t + 1:t + 2, :] = k_t[t][bi:bi + 1, sl]
            cols = mt_ref[...].T
            sn = s0 * cols[:, 0:1]
            for t in range(L):
                sn = sn + cols[:, t + 1:t + 2] * v_t[t][bi:bi + 1, sl]
            snew_ref[bi, h] = sn
    for t in range(L):
        oacc_ref[rows[t], :] += ostage_ref[t]

    @pl.when(step == pl.num_programs(0) - 1)
    def _():
        gate = _silu(og_ref[...])
        ng = ng_ref[...]
        for h in range(HH):
            sl = slice(h * DK, (h + 1) * DK)
            o_ref[:, sl] = (_rms(oacc_ref[:, sl], ng) * gate[:, sl]).astype(bf16)


def _hgrn_sample(l, proj, lb, onorm_g, state):
    def slab(k):
        return pl.BlockSpec((TS, KW), lambda s: (TP // TS, k))

    return pl.pallas_call(
        _hgrn_sample_kernel,
        grid=(DEC_BATCH // SBB,),
        in_specs=[slab(OFF_HQ // KW), slab(OFF_HF // KW), slab(OFF_HV // KW), slab(OFF_HG // KW),
                  pl.BlockSpec((1, KW), lambda s: (0, 0)),
                  pl.BlockSpec((1, DV), lambda s: (0, 0)),
                  pl.BlockSpec((1, SBB, HH, DK, DV), lambda s: (l, s, 0, 0, 0))],
        out_specs=[pl.BlockSpec((TS, KW), lambda s: (0, 0)),
                   pl.BlockSpec((SBB, HH, DK, DV), lambda s: (s, 0, 0, 0))],
        out_shape=[jax.ShapeDtypeStruct((TS, KW), bf16),
                   jax.ShapeDtypeStruct((DEC_BATCH, HH, DK, DV), f32)],
        scratch_shapes=[pltpu.VMEM((TS, KW), f32), pltpu.VMEM((TS, KW), f32),
                        pltpu.VMEM((DEC_BATCH, KW), f32), pltpu.VMEM((TS, KW), f32),
                        pltpu.VMEM((LANES, DK), f32), pltpu.VMEM((2 * SUBLANES, DK), f32),
                        pltpu.VMEM((DEC_SEQ, SBB, KW), f32)],
        compiler_params=_cparams(("arbitrary",), 56),
        name="hgrn_sample",
    )(proj, proj, proj, proj, lb.reshape(1, KW), onorm_g.reshape(1, DV), state)


def _head_rms(x, g, n):
    return jnp.concatenate([_rms(x[:, i * HD:(i + 1) * HD], g) for i in range(n)], axis=1)


def _sink_softmax_pv(s, sink_col, vb):
    m = jnp.maximum(jnp.max(s, axis=-1, keepdims=True), sink_col)
    p = jnp.exp(s - m)
    den = jnp.sum(p, axis=-1, keepdims=True) + jnp.exp(sink_col - m)
    return _dot(p.astype(bf16), vb) / den


def _attn_prompt_kernel(sink_ref, q_ref, kc_ref, kp_ref, vc_ref, vp_ref, qg_ref, kg_ref, bias_ref,
                        o_ref, kn_ref):
    i = pl.program_id(1)
    kc = _head_rms(kc_ref[...], kg_ref[...], NKV)
    kp = _head_rms(kp_ref[...], kg_ref[...], NKV)
    kn_ref[...] = kc
    kk = jnp.concatenate([kp, kc], axis=0).astype(bf16)
    vv = jnp.concatenate([vp_ref[...], vc_ref[...]], axis=0).astype(bf16)
    q = q_ref[...]
    col = lax.broadcasted_iota(i32, (GQ * WINDOW, 2 * WINDOW), 1)
    valid = (i > 0) | (col >= WINDOW)
    for n in range(NKV):
        qn = jnp.concatenate(
            [_rms(q[:, (n * GQ + g) * HD:(n * GQ + g + 1) * HD], qg_ref[...]) for g in range(GQ)], axis=0)
        s = _dot_nt(qn.astype(bf16), kk[:, n * HD:(n + 1) * HD]) * (HD ** -0.5)
        s = s + bias_ref[n * GQ:(n + 1) * GQ].reshape(GQ * WINDOW, 2 * WINDOW)
        s = jnp.where(valid, s, NEG)
        sink_col = jnp.concatenate(
            [jnp.full((WINDOW, 1), sink_ref[n * GQ + g], f32) for g in range(GQ)], axis=0)
        o = _sink_softmax_pv(s, sink_col, vv[:, n * HD:(n + 1) * HD])
        for g in range(GQ):
            hh = n * GQ + g
            o_ref[:, hh * HD:(hh + 1) * HD] = o[g * WINDOW:(g + 1) * WINDOW].astype(bf16)


def _attn_prompt(proj, sinks, q_norm_g, k_norm_g, bias):
    nb = SEQ // WINDOW
    kblk, vblk = OFF_AK // KVW, OFF_AV // KVW

    def cur(col):
        return lambda b, i: (b * nb + i, col)

    def prev(col):
        return lambda b, i: (b * nb + jnp.maximum(i - 1, 0), col)

    return pl.pallas_call(
        _attn_prompt_kernel,
        grid=(BATCH, nb),
        in_specs=[pl.BlockSpec(memory_space=pltpu.SMEM),
                  pl.BlockSpec((WINDOW, NH * HD), cur(OFF_AQ // (NH * HD))),
                  pl.BlockSpec((WINDOW, KVW), cur(kblk)),
                  pl.BlockSpec((WINDOW, KVW), prev(kblk)),
                  pl.BlockSpec((WINDOW, KVW), cur(vblk)),
                  pl.BlockSpec((WINDOW, KVW), prev(vblk)),
                  pl.BlockSpec((1, HD), lambda b, i: (0, 0)),
                  pl.BlockSpec((1, HD), lambda b, i: (0, 0)),
                  pl.BlockSpec((NH, WINDOW, 2 * WINDOW), lambda b, i: (0, 0, 0))],
        out_specs=[pl.BlockSpec((WINDOW, NH * HD), lambda b, i: (b * nb + i, 0)),
                   pl.BlockSpec((WINDOW, KVW), lambda b, i: (b * nb + i, 0))],
        out_shape=[jax.ShapeDtypeStruct((TP, NH * HD), bf16),
                   jax.ShapeDtypeStruct((TP, KVW), f32)],
        compiler_params=_cparams(("arbitrary", "arbitrary"), 32),
        name="attn_prompt",
    )(sinks, proj, proj, proj, proj, proj, q_norm_g.reshape(1, HD), k_norm_g.reshape(1, HD), bias)


def _attn_sample_kernel(sink_ref, q_ref, k_ref, v_ref, ck_ref, cv_ref, qg_ref, kg_ref, bias_ref,
                        o_ref, kw_ref, vw_ref):
    pad_rows = 2 * WINDOW - WINDOW - SPAD
    for bi in range(SBB):
        kn = _head_rms(k_ref[:, bi, :], kg_ref[...], NKV)
        vn = v_ref[:, bi, :]
        ck = ck_ref[0, bi]
        cv = cv_ref[0, bi]
        kw_ref[bi, 0:WINDOW - DEC_SEQ, :] = ck[DEC_SEQ:, :]
        kw_ref[bi, WINDOW - DEC_SEQ:WINDOW, :] = kn[0:DEC_SEQ]
        vw_ref[bi, 0:WINDOW - DEC_SEQ, :] = cv[DEC_SEQ:, :]
        vw_ref[bi, WINDOW - DEC_SEQ:WINDOW, :] = vn[0:DEC_SEQ]
        zpad = jnp.zeros((pad_rows, KVW), f32)
        kk = jnp.concatenate([ck, kn, zpad], axis=0).astype(bf16)
        vv = jnp.concatenate([cv, vn, zpad], axis=0).astype(bf16)
        q = q_ref[:, bi, :]
        for n in range(NKV):
            qn = jnp.concatenate(
                [_rms(q[:, (n * GQ + g) * HD:(n * GQ + g + 1) * HD], qg_ref[...]) for g in range(GQ)], axis=0)
            s = _dot_nt(qn.astype(bf16), kk[:, n * HD:(n + 1) * HD]) * (HD ** -0.5)
            s = s + bias_ref[n * GQ:(n + 1) * GQ].reshape(GQ * SPAD, 2 * WINDOW)
            sink_col = jnp.concatenate(
                [jnp.full((SPAD, 1), sink_ref[n * GQ + g], f32) for g in range(GQ)], axis=0)
            o = _sink_softmax_pv(s, sink_col, vv[:, n * HD:(n + 1) * HD])
            for g in range(GQ):
                hh = n * GQ + g
                o_ref[:, bi, hh * HD:(hh + 1) * HD] = o[g * SPAD:(g + 1) * SPAD].astype(bf16)


def _attn_sample(l, qkv_pad, cache_k, cache_v, sinks, q_norm_g, k_norm_g, bias):
    kblk = NH * HD // KVW
    return pl.pallas_call(
        _attn_sample_kernel,
        grid=(DEC_BATCH // SBB,),
        in_specs=[pl.BlockSpec(memory_space=pltpu.SMEM),
                  pl.BlockSpec((SPAD, SBB, NH * HD), lambda s: (0, s, 0)),
                  pl.BlockSpec((SPAD, SBB, KVW), lambda s: (0, s, kblk)),
                  pl.BlockSpec((SPAD, SBB, KVW), lambda s: (0, s, kblk + 1)),
                  pl.BlockSpec((1, SBB, WINDOW, KVW), lambda s: (l, s, 0, 0)),
                  pl.BlockSpec((1, SBB, WINDOW, KVW), lambda s: (l, s, 0, 0)),
                  pl.BlockSpec((1, HD), lambda s: (0, 0)),
                  pl.BlockSpec((1, HD), lambda s: (0, 0)),
                  pl.BlockSpec((NH, SPAD, 2 * WINDOW), lambda s: (0, 0, 0))],
        out_specs=[pl.BlockSpec((SPAD, SBB, NH * HD), lambda s: (0, s, 0)),
                   pl.BlockSpec((SBB, WINDOW, KVW), lambda s: (s, 0, 0)),
                   pl.BlockSpec((SBB, WINDOW, KVW), lambda s: (s, 0, 0))],
        out_shape=[jax.ShapeDtypeStruct((SPAD, DEC_BATCH, NH * HD), bf16),
                   jax.ShapeDtypeStruct((DEC_BATCH, WINDOW, KVW), f32),
                   jax.ShapeDtypeStruct((DEC_BATCH, WINDOW, KVW), f32)],
        compiler_params=_cparams(("arbitrary",), 32),
        name="attn_sample",
    )(sinks, qkv_pad, qkv_pad, qkv_pad, cache_k, cache_v, q_norm_g.reshape(1, HD), k_norm_g.reshape(1, HD), bias)


def _merge_kernel(oh_ref, oa_ref, ga_ref, gb_ref, x_ref, g1p_ref, g1s_ref, wa_ref, wb_ref, wo_ref,
                  o_ref, acc_ref):
    i, j = pl.program_id(0), pl.program_id(1)
    merged = (jax.nn.sigmoid(ga_ref[...]) * _dot(oh_ref[...], wa_ref[0])
              + jax.nn.sigmoid(gb_ref[...]) * _dot(oa_ref[...], wb_ref[0]))
    part = _dot(merged.astype(bf16), wo_ref[0])

    @pl.when(j == 0)
    def _():
        acc_ref[...] = part

    @pl.when(j > 0)
    def _():
        acc_ref[...] += part

    @pl.when(j == pl.num_programs(1) - 1)
    def _():
        g1 = _mod_rows(i >= NPT, g1p_ref, g1s_ref, TM)
        o_ref[...] = x_ref[...] + g1 * acc_ref[...]


def _merge(l, o_h, o_a, proj, x, mod3, mod2, wa_bf, wb_bf, wo_bf):
    g1p, g1s = _mod_specs(2, TPB, NPT)
    return pl.pallas_call(
        _merge_kernel,
        grid=(NT, D // TN_MG),
        in_specs=[pl.BlockSpec((TM, KW), lambda i, j: (i, 0)),
                  pl.BlockSpec((TM, NH * HD), lambda i, j: (i, 0)),
                  pl.BlockSpec((TM, TN_MG), lambda i, j: (i, OFF_GA // TN_MG + j)),
                  pl.BlockSpec((TM, TN_MG), lambda i, j: (i, OFF_GB // TN_MG + j)),
                  pl.BlockSpec((TM, D), lambda i, j: (i, 0)),
                  g1p, g1s,
                  pl.BlockSpec((1, KW, TN_MG), lambda i, j: (l, 0, j)),
                  pl.BlockSpec((1, NH * HD, TN_MG), lambda i, j: (l, 0, j)),
                  pl.BlockSpec((1, TN_MG, D), lambda i, j: (l, j, 0))],
        out_specs=pl.BlockSpec((TM, D), lambda i, j: (i, 0)),
        out_shape=jax.ShapeDtypeStruct((T, D), f32),
        scratch_shapes=[pltpu.VMEM((TM, D), f32)],
        compiler_params=_cparams(("arbitrary", "arbitrary"), 48),
        name="merge",
    )(o_h, o_a, proj, proj, x, mod3, mod2, wa_bf, wb_bf, wo_bf)


def _first_index(hit, idx, big, axis):
    return jnp.min(jnp.where(hit, idx, big), axis=axis, keepdims=True)


def _pack_bf16_pairs(h):
    lo = lax.bitcast_convert_type(h[:, :D // 2].astype(bf16).astype(f32), u32)
    hi = lax.bitcast_convert_type(h[:, D // 2:].astype(bf16).astype(f32), u32)
    return (hi & jnp.uint32(0xFFFF0000)) | (lo >> 16)


def _unpack_bf16_pairs(p):
    lo = lax.bitcast_convert_type(p << 16, f32).astype(bf16)
    hi = lax.bitcast_convert_type(p & jnp.uint32(0xFFFF0000), f32).astype(bf16)
    return lo, hi


def _router_kernel(x_ref, g_ref, scp_ref, scs_ref, shp_ref, shs_ref, wr_ref, rb_ref,
                   hp_ref, ids_ref, wts_ref, rank_ref, cnt_ref):
    i = pl.program_id(0)
    y = _rms(x_ref[...], g_ref[...])
    sc = _mod_rows(i >= NPT, scp_ref, scs_ref, TM)
    sh = _mod_rows(i >= NPT, shp_ref, shs_ref, TM)
    h = y * (1.0 + sc) + sh
    hp_ref[...] = _pack_bf16_pairs(h).reshape(TM, ROW_TILES_X, LANES)
    wr = wr_ref[0]
    w1 = wr.astype(bf16)
    w2 = (wr - w1.astype(f32)).astype(bf16)
    h1 = h.astype(bf16)
    h2 = (h - h1.astype(f32)).astype(bf16)
    logits = _dot_nt(w1, h1) + (_dot_nt(w1, h2) + _dot_nt(w2, h1))
    scores = jax.nn.sigmoid(logits)
    biased = scores + rb_ref[...]
    b3 = biased.reshape(NG, EPG, TM)
    e_in = lax.broadcasted_iota(i32, (NG, EPG, TM), 1).astype(f32)
    m1 = jnp.max(b3, axis=1, keepdims=True)
    first = _first_index(b3 == m1, e_in, float(EPG), 1)
    m2 = jnp.max(jnp.where(e_in == first, -jnp.inf, b3), axis=1, keepdims=True)
    gs = (m1 + m2).reshape(NG, TM)
    g_idx = lax.broadcasted_iota(i32, (NG, TM), 0).astype(f32)
    gsel = jnp.zeros((NG, TM), f32)
    for _ in range(TOPKG):
        m = jnp.max(gs, axis=0, keepdims=True)
        pick = g_idx == _first_index(gs == m, g_idx, float(NG), 0)
        gsel = jnp.where(pick, 1.0, gsel)
        gs = jnp.where(pick, -jnp.inf, gs)
    emask = jnp.broadcast_to(gsel.reshape(NG, 1, TM), (NG, EPG, TM)).reshape(NE, TM)
    cur = jnp.where(emask > 0.5, biased, NEG)
    e_idx = lax.broadcasted_iota(i32, (NE, TM), 0).astype(f32)
    ids, ws, picks = [], [], []
    for _ in range(TOPK):
        m = jnp.max(cur, axis=0, keepdims=True)
        first = _first_index(cur == m, e_idx, float(NE), 0)
        pick = e_idx == first
        ids.append(first)
        picks.append(pick)
        ws.append(jnp.sum(jnp.where(pick, scores, 0.0), axis=0, keepdims=True))
        cur = jnp.where(pick, -jnp.inf, cur)
    w = jnp.concatenate(ws, axis=0)
    ids_ref[...] = jnp.concatenate(ids, axis=0).astype(i32)
    wts_ref[...] = w / jnp.sum(w, axis=0, keepdims=True) * ROUTED_SCALE

    @pl.when(i == 0)
    def _():
        cnt_ref[...] = jnp.zeros_like(cnt_ref)

    onehot = jnp.zeros((NE, TM), f32)
    for pick in picks:
        onehot = jnp.where(pick, 1.0, onehot)
    before = (lax.broadcasted_iota(i32, (TM, TM), 0) < lax.broadcasted_iota(i32, (TM, TM), 1)).astype(bf16)
    base = cnt_ref[:, 0:1] + _dot(onehot.astype(bf16), before)
    rank_ref[...] = jnp.concatenate(
        [jnp.sum(jnp.where(pick, base, 0.0), axis=0, keepdims=True) for pick in picks], axis=0).astype(i32)
    cnt_ref[...] += jnp.sum(onehot, axis=1, keepdims=True)


def _router(l, x1, norm_g, mod3, mod2, w_router_t, router_bias):
    scp, scs = _mod_specs(4, TPB, NPT)
    shp, shs = _mod_specs(3, TPB, NPT)
    return pl.pallas_call(
        _router_kernel,
        grid=(NT,),
        in_specs=[pl.BlockSpec((TM, D), lambda i: (i, 0)),
                  pl.BlockSpec((1, D), lambda i: (0, 0)),
                  scp, scs, shp, shs,
                  pl.BlockSpec((1, NE, D), lambda i: (l, 0, 0)),
                  pl.BlockSpec((NE, 1), lambda i: (0, 0))],
        out_specs=[pl.BlockSpec((TM, ROW_TILES_X, LANES), lambda i: (i, 0, 0)),
                   pl.BlockSpec((TOPK, TM), lambda i: (0, i)),
                   pl.BlockSpec((TOPK, TM), lambda i: (0, i)),
                   pl.BlockSpec((TOPK, TM), lambda i: (0, i)),
                   pl.BlockSpec((NE, LANES), lambda i: (0, 0))],
        out_shape=[jax.ShapeDtypeStruct((T, ROW_TILES_X, LANES), u32),
                   jax.ShapeDtypeStruct((TOPK, T), i32),
                   jax.ShapeDtypeStruct((TOPK, T), f32),
                   jax.ShapeDtypeStruct((TOPK, T), i32),
                   jax.ShapeDtypeStruct((NE, LANES), f32)],
        compiler_params=_cparams(("arbitrary",), 48),
        name="router",
    )(x1, norm_g.reshape(1, D), mod3, mod2, mod3, mod2, w_router_t, router_bias.reshape(NE, 1))


def _block_tables(cnt):
    counts = cnt[:, 0].astype(i32)
    padded = (counts + BLK - 1) // BLK * BLK
    pend = jnp.cumsum(padded)
    pstart = pend - padded
    blk_start = jnp.arange(NBLK, dtype=i32) * BLK
    block_e = jnp.minimum(jnp.sum(blk_start[:, None] >= pend[None, :], axis=1), NE - 1).astype(i32)
    n_used = (pend[-1] // BLK).astype(i32).reshape(1)
    after = (pend // BLK).astype(i32)
    next_of_expert = jnp.where(after < n_used[0], block_e[jnp.minimum(after, NBLK - 1)], -1)
    next_e = next_of_expert[block_e].astype(i32)
    return (block_e, n_used, next_e, pstart.astype(i32), (pstart + counts).astype(i32),
            (padded - counts).astype(i32))


def _pos_kernel(ps_ref, ids_ref, rank_ref, pos_ref):
    ids = ids_ref[...]
    acc = rank_ref[...]
    for e in range(NE):
        acc = acc + jnp.where(ids == e, ps_ref[e], 0)
    pos_ref[...] = acc


def _slot_positions(pstart, ids_t, rank_t):
    return pl.pallas_call(
        _pos_kernel,
        grid=(1,),
        in_specs=[pl.BlockSpec(memory_space=pltpu.SMEM),
                  pl.BlockSpec((TOPK, T), lambda i: (0, 0)),
                  pl.BlockSpec((TOPK, T), lambda i: (0, 0))],
        out_specs=pl.BlockSpec((TOPK, T), lambda i: (0, 0)),
        out_shape=jax.ShapeDtypeStruct((TOPK, T), i32),
        compiler_params=_cparams(("arbitrary",), 16),
        name="slot_positions",
    )(pstart, ids_t, rank_t)


def _dispatch_kernel(pad0_ref, padn_ref, nu_ref, pos_ref, hp_hbm, xs_hbm, zero_ref, sem):
    i = pl.program_id(0)

    def tok_copy(k, j, u):
        r = j * SUBLANES + u
        return pltpu.make_async_copy(hp_hbm.at[i * TMF + r], xs_hbm.at[pos_ref[0, 0, k * TMF + r]], sem)

    def pad_copy(e, r):
        return pltpu.make_async_copy(zero_ref.at[0], xs_hbm.at[pad0_ref[e] + r], sem)

    def tail_copy(b):
        return pltpu.make_async_copy(zero_ref, xs_hbm.at[pl.ds(b * BLK, BLK)], sem)

    def for_tokens(fn):
        for k in range(TOPK):
            def body(j, c):
                for u in range(SUBLANES):
                    fn(tok_copy(k, j, u), u)
                return c
            lax.fori_loop(0, TMF // SUBLANES, body, 0)

    def for_pads(fn):
        def per_expert(e, c):
            def body(r, c2):
                fn(pad_copy(e, r), 0)
                return c2
            return lax.fori_loop(0, padn_ref[e], body, c)
        lax.fori_loop(0, NE, per_expert, 0)

        def per_block(b, c):
            fn(tail_copy(b), 0)
            return c
        lax.fori_loop(nu_ref[0], NBLK, per_block, 0)

    @pl.when(i == 0)
    def _():
        zero_ref[...] = jnp.zeros_like(zero_ref)
        for_pads(lambda cp, u: cp.start(priority=u % 2))

    for_tokens(lambda cp, u: cp.start(priority=u % 2))
    for_tokens(lambda cp, u: cp.wait())

    @pl.when(i == 0)
    def _():
        for_pads(lambda cp, u: cp.wait())


def _dispatch(pad_start, pad_cnt, n_used, pos_tiles, hp):
    grid_spec = pltpu.PrefetchScalarGridSpec(
        num_scalar_prefetch=3,
        grid=(T // TMF,),
        in_specs=[pl.BlockSpec((1, 1, TOPK * TMF), lambda i, p0, pn, nu: (i, 0, 0), memory_space=pltpu.SMEM),
                  pl.BlockSpec(memory_space=pl.ANY)],
        out_specs=pl.BlockSpec(memory_space=pl.ANY),
        scratch_shapes=[pltpu.VMEM((BLK, ROW_TILES_X, LANES), u32), pltpu.SemaphoreType.DMA(())],
    )
    return pl.pallas_call(
        _dispatch_kernel,
        grid_spec=grid_spec,
        out_shape=jax.ShapeDtypeStruct((NROW, ROW_TILES_X, LANES), u32),
        compiler_params=_cparams(("arbitrary",), 16),
        name="dispatch",
    )(pad_start, pad_cnt, n_used, pos_tiles, hp)


def _expert_kernel(l, be_ref, nu_ref, nx_ref, x_ref, wg_hbm, wu_hbm, wd_hbm, y_ref,
                   wg_f, wu_f, wd_f, wsem, ord_ref, wgb, wub, wdb):
    i = pl.program_id(0)

    def weight_copies(e, s):
        return (pltpu.make_async_copy(wg_hbm.at[l, e], wg_f.at[s], wsem.at[s]),
                pltpu.make_async_copy(wu_hbm.at[l, e], wu_f.at[s], wsem.at[s]),
                pltpu.make_async_copy(wd_hbm.at[l, e], wd_f.at[s], wsem.at[s]))

    @pl.when(i == 0)
    def _():
        ord_ref[0] = 0
        for cp in weight_copies(be_ref[0], 0):
            cp.start()

    @pl.when(i >= nu_ref[0])
    def _():
        y_ref[...] = jnp.zeros_like(y_ref)

    @pl.when(i < nu_ref[0])
    def _():
        @pl.when((i == 0) | (be_ref[i] != be_ref[jnp.maximum(i - 1, 0)]))
        def _():
            s = ord_ref[0] % 2
            for cp in weight_copies(be_ref[i], s):
                cp.wait()
            wgb[...] = wg_f[s].astype(bf16)
            wub[...] = wu_f[s].astype(bf16)
            wdb[...] = wd_f[s].astype(bf16)

            @pl.when(nx_ref[i] >= 0)
            def _():
                for cp in weight_copies(nx_ref[i], 1 - s):
                    cp.start()

            ord_ref[0] = ord_ref[0] + 1

        lo, hi = _unpack_bf16_pairs(x_ref[...].reshape(BLK, D // 2))
        half = D // 2
        g = _dot(lo, wgb[0:half, :]) + _dot(hi, wgb[half:D, :])
        u = _dot(lo, wub[0:half, :]) + _dot(hi, wub[half:D, :])
        y = _dot((_silu(g) * u).astype(bf16), wdb[...])
        y_ref[...] = _pack_bf16_pairs(y).reshape(BLK, ROW_TILES_X, LANES)


def _experts(l, xs, block_e, n_used, next_e, w_gate, w_up, w_down):
    grid_spec = pltpu.PrefetchScalarGridSpec(
        num_scalar_prefetch=3,
        grid=(NBLK,),
        in_specs=[pl.BlockSpec((BLK, ROW_TILES_X, LANES), lambda i, be, nu, nx: (jnp.minimum(i, nu[0] - 1), 0, 0)),
                  pl.BlockSpec(memory_space=pl.ANY),
                  pl.BlockSpec(memory_space=pl.ANY),
                  pl.BlockSpec(memory_space=pl.ANY)],
        out_specs=pl.BlockSpec((BLK, ROW_TILES_X, LANES), lambda i, be, nu, nx: (i, 0, 0)),
        scratch_shapes=[pltpu.VMEM((2, D, DE), f32), pltpu.VMEM((2, D, DE), f32), pltpu.VMEM((2, DE, D), f32),
                        pltpu.SemaphoreType.DMA((2,)), pltpu.SMEM((1,), i32),
                        pltpu.VMEM((D, DE), bf16), pltpu.VMEM((D, DE), bf16), pltpu.VMEM((DE, D), bf16)],
    )
    return pl.pallas_call(
        functools.partial(_expert_kernel, l),
        grid_spec=grid_spec,
        out_shape=jax.ShapeDtypeStruct((NROW, ROW_TILES_X, LANES), u32),
        compiler_params=_cparams(("arbitrary",), 56),
        name="experts",
    )(block_e, n_used, next_e, xs, w_gate, w_up, w_down)


def _final_kernel(pos_ref, posn_ref, x_ref, hp_ref, w_ref, g2p_ref, g2s_ref, sg_ref, su_ref, sd_ref,
                  y_hbm, o_ref, ybuf, sem):
    i = pl.program_id(0)
    n = pl.num_programs(0)
    slot = i % 2

    def row_copy(pref, k, j, u, s):
        r = j * SUBLANES + u
        return pltpu.make_async_copy(y_hbm.at[pref[0, 0, k * TMF + r]], ybuf.at[s, k, r], sem.at[s])

    def for_rows(pref, s, fn):
        for k in range(TOPK):
            def body(j, c):
                for u in range(SUBLANES):
                    fn(row_copy(pref, k, j, u, s), u)
                return c
            lax.fori_loop(0, TMF // SUBLANES, body, 0)

    @pl.when(i == 0)
    def _():
        for_rows(pos_ref, 0, lambda cp, u: cp.start(priority=u % 2))

    @pl.when(i + 1 < n)
    def _():
        for_rows(posn_ref, 1 - slot, lambda cp, u: cp.start(priority=u % 2))

    lo, hi = _unpack_bf16_pairs(hp_ref[...].reshape(TMF, D // 2))
    hb = jnp.concatenate([lo, hi], axis=1)
    shared = _dot((_silu(_dot(hb, sg_ref[0])) * _dot(hb, su_ref[0])).astype(bf16), sd_ref[0])

    for_rows(pos_ref, slot, lambda cp, u: cp.wait())

    w = w_ref[...]
    r_lo = jnp.zeros((TMF, D // 2), f32)
    r_hi = jnp.zeros((TMF, D // 2), f32)
    for k in range(TOPK):
        p = ybuf[slot, k].reshape(TMF, D // 2)
        r_lo = r_lo + w[:, k:k + 1] * lax.bitcast_convert_type(p << 16, f32)
        r_hi = r_hi + w[:, k:k + 1] * lax.bitcast_convert_type(p & jnp.uint32(0xFFFF0000), f32)
    routed = jnp.concatenate([r_lo, r_hi], axis=1)
    g2 = _mod_rows(i >= TP // TMF, g2p_ref, g2s_ref, TMF)
    o_ref[...] = x_ref[...] + g2 * (shared + routed)


def _final(l, x1, hp, wts, pos_tiles, mod3, mod2, sg_bf, su_bf, sd_bf, yb):
    g2p, g2s = _mod_specs(5, SEQ // TMF, TP // TMF)
    ntile = T // TMF
    return pl.pallas_call(
        _final_kernel,
        grid=(ntile,),
        in_specs=[pl.BlockSpec((1, 1, TOPK * TMF), lambda i: (i, 0, 0), memory_space=pltpu.SMEM),
                  pl.BlockSpec((1, 1, TOPK * TMF), lambda i: (jnp.minimum(i + 1, ntile - 1), 0, 0),
                               memory_space=pltpu.SMEM),
                  pl.BlockSpec((TMF, D), lambda i: (i, 0)),
                  pl.BlockSpec((TMF, ROW_TILES_X, LANES), lambda i: (i, 0, 0)),
                  pl.BlockSpec((TMF, TOPK), lambda i: (i, 0)),
                  g2p, g2s,
                  pl.BlockSpec((1, D, DE), lambda i: (l, 0, 0)),
                  pl.BlockSpec((1, D, DE), lambda i: (l, 0, 0)),
                  pl.BlockSpec((1, DE, D), lambda i: (l, 0, 0)),
                  pl.BlockSpec(memory_space=pl.ANY)],
        out_specs=pl.BlockSpec((TMF, D), lambda i: (i, 0)),
        out_shape=jax.ShapeDtypeStruct((T, D), f32),
        scratch_shapes=[pltpu.VMEM((2, TOPK, TMF, ROW_TILES_X, LANES), u32),
                        pltpu.SemaphoreType.DMA((2,))],
        compiler_params=_cparams(("arbitrary",), 56),
        name="moe_combine",
    )(pos_tiles, pos_tiles, x1, hp, wts, mod3, mod2, sg_bf, su_bf, sd_bf, yb)


def _layer(l, x, mod, lb, state, cache_k, cache_v, bias_p, bias_s, norm1_g, norm2_g, w_in_bf, onorm_g,
           q_norm_g, k_norm_g, sinks, wa_bf, wb_bf, wo_bf, w_router_t, router_bias,
           w_eg, w_eu, w_ed, sg_bf, su_bf, sd_bf):
    mod3 = mod.reshape(mod.shape[0], 1, 6 * D)
    proj = _in_proj(l, x, norm1_g, mod3, mod, w_in_bf)

    oh_p, s_p = _hgrn_prompt(proj, lb, onorm_g)
    oh_s, s_s = _hgrn_sample(l, proj, lb, onorm_g, state)

    oa_p, kn_p = _attn_prompt(proj, sinks, q_norm_g, k_norm_g, bias_p)
    qkv_pad = jnp.pad(proj[TP:, OFF_AQ:OFF_GA].reshape(DEC_SEQ, DEC_BATCH, OFF_GA - OFF_AQ),
                      ((0, SPAD - DEC_SEQ), (0, 0), (0, 0)))
    oa_s, kw_s, vw_s = _attn_sample(l, qkv_pad, cache_k, cache_v, sinks, q_norm_g, k_norm_g, bias_s)

    o_h = jnp.concatenate([oh_p, oh_s], axis=0)
    o_a = jnp.concatenate([oa_p, oa_s[:DEC_SEQ].reshape(TS, NH * HD)], axis=0)
    x1 = _merge(l, o_h, o_a, proj, x, mod3, mod, wa_bf, wb_bf, wo_bf)

    hp, ids_t, wts_t, rank_t, cnt = _router(l, x1, norm2_g, mod3, mod, w_router_t, router_bias)
    block_e, n_used, next_e, pstart, pad_start, pad_cnt = _block_tables(cnt)
    pos = _slot_positions(pstart, ids_t, rank_t)
    pos_tiles = pos.reshape(TOPK, T // TMF, TMF).transpose(1, 0, 2).reshape(T // TMF, 1, TOPK * TMF)
    xs = _dispatch(pad_start, pad_cnt, n_used, pos_tiles, hp)
    yb = _experts(l, xs, block_e, n_used, next_e, w_eg, w_eu, w_ed)
    x2 = _final(l, x1, hp, wts_t.T, pos_tiles, mod3, mod, sg_bf, su_bf, sd_bf, yb)

    k_p = kn_p.reshape(BATCH, SEQ, NKV, HD)[:, SEQ - WINDOW:]
    v_p = proj[:TP, OFF_AV:OFF_GA].reshape(BATCH, SEQ, NKV, HD)[:, SEQ - WINDOW:]
    return (x2, s_p, s_s, k_p, v_p, kw_s.reshape(DEC_BATCH, WINDOW, NKV, HD),
            vw_s.reshape(DEC_BATCH, WINDOW, NKV, HD))


def kernel(x_prompt, x_sample, state_hgrn, cache_k_win, cache_v_win, c_prompt, c_sample, norm1_g, norm2_g,
           w_ada, b_ada, w_in, hgrn_lb_logits, hgrn_onorm_g, q_norm_g, k_norm_g, attn_sinks, rel_bias,
           w_branch_a, w_branch_b, w_out, w_router, router_bias, w_exp_gate, w_exp_up, w_exp_down,
           w_sh_gate, w_sh_up, w_sh_down):
    p_lb = jax.nn.softmax(hgrn_lb_logits.astype(f32), axis=0)
    lower_bounds = jnp.clip(jnp.cumsum(p_lb, axis=0) - p_lb[0], 0.0, 1.0 - 1e-6)

    c_all = jnp.concatenate([c_sample, c_prompt, jnp.zeros((SUBLANES - BATCH, D), f32)], axis=0)
    mod = _ada_mod(c_all, w_ada, b_ada)
    bias_p = _bias_table(rel_bias, _prompt_buckets())
    bias_s = _bias_table(rel_bias, _sample_buckets())

    x = jnp.concatenate([x_prompt.reshape(TP, D), x_sample.transpose(1, 0, 2).reshape(TS, D)], axis=0)
    cache_k = cache_k_win.reshape(DEPTH, DEC_BATCH, WINDOW, KVW)
    cache_v = cache_v_win.reshape(DEPTH, DEC_BATCH, WINDOW, KVW)
    w_in_bf, wa_bf, wb_bf, wo_bf = (w.astype(bf16) for w in (w_in, w_branch_a, w_branch_b, w_out))
    sg_bf, su_bf, sd_bf = (w.astype(bf16) for w in (w_sh_gate, w_sh_up, w_sh_down))
    w_router_t = w_router.transpose(0, 2, 1)
    sp_l, ss_l, kp_l, vp_l, ks_l, vs_l = [], [], [], [], [], []
    for l in range(DEPTH):
        x, s_p, s_s, k_p, v_p, k_s, v_s = _layer(
            l, x, mod[l], lower_bounds[l], state_hgrn, cache_k, cache_v, bias_p, bias_s,
            norm1_g[l], norm2_g[l], w_in_bf, hgrn_onorm_g[l], q_norm_g[l], k_norm_g[l], attn_sinks[l],
            wa_bf, wb_bf, wo_bf, w_router_t, router_bias[l],
            w_exp_gate, w_exp_up, w_exp_down, sg_bf, su_bf, sd_bf)
        sp_l.append(s_p); ss_l.append(s_s); kp_l.append(k_p); vp_l.append(v_p); ks_l.append(k_s); vs_l.append(v_s)

    y_prompt = x[:TP].reshape(BATCH, SEQ, D)
    y_sample = x[TP:].reshape(DEC_SEQ, DEC_BATCH, D).transpose(1, 0, 2)
    return (y_prompt, y_sample, jnp.stack(sp_l), jnp.stack(ss_l), jnp.stack(kp_l), jnp.stack(vp_l),
            jnp.stack(ks_l), jnp.stack(vs_l))
```

```python
import functools
import math

import numpy as np
import jax
import jax.numpy as jnp
from jax import lax
from jax.experimental import pallas as pl
from jax.experimental.pallas import tpu as pltpu

f32 = jnp.float32
bf16 = jnp.bfloat16
i32 = jnp.int32
u32 = jnp.uint32

D = 2048
BATCH, SEQ = 2, 4096
DEPTH = 2
DEC_BATCH, DEC_SEQ = 128, 4
HH, DK, DV = 8, 128, 128
KW = HH * DK
NH, NKV, HD = 16, 4, 64
GQ = NH // NKV
KVW = NKV * HD
WINDOW = 128
NUM_BUCKETS, MAX_DISTANCE = 32, 128
NE, NG, TOPKG, TOPK, DE = 64, 8, 4, 8, 512
EPG = NE // NG
ROUTED_SCALE = 2.5
IN_WIDTH = 2 * KW + 2 * KW + NH * HD + 2 * KVW + 2 * D
EPS = 1e-6
NEG = -1e30
TINY = 1e-30

OFF_HQ, OFF_HF, OFF_HV, OFF_HG = 0, KW, 2 * KW, 3 * KW
OFF_AQ = 4 * KW
OFF_AK = OFF_AQ + NH * HD
OFF_AV = OFF_AK + KVW
OFF_GA = OFF_AV + KVW
OFF_GB = OFF_GA + D

TP = BATCH * SEQ
TS = DEC_BATCH * DEC_SEQ
T = TP + TS
TM = 512
NPT = TP // TM
NT = T // TM
TPB = SEQ // TM
TN_IN = 2432
TN_MG = 512
TMF = 256
HC = 256
HGRN_WIDTHS = (128, 64, 32, 16, 8, 4)
SBB = 8
SPAD = 8
BLK = 256
NA = T * TOPK
NBLK = NA // BLK + NE
NROW = NBLK * BLK

V7X_VMEM_BYTES = 64 * 1024 * 1024
SUBLANES, LANES = 8, 128
ROW_TILES_X = D // 2 // LANES


def _cparams(sem, vmem_mib):
    assert vmem_mib * 1024 * 1024 < V7X_VMEM_BYTES
    return pltpu.CompilerParams(dimension_semantics=sem, vmem_limit_bytes=vmem_mib * 1024 * 1024)


def _silu(x):
    return x * jax.nn.sigmoid(x)


def _rms(x, g):
    return x * lax.rsqrt(jnp.mean(x * x, axis=-1, keepdims=True) + EPS) * g


def _dot(a, b):
    return jnp.dot(a, b, preferred_element_type=f32)


def _dot_nt(a, b):
    return lax.dot_general(a, b, (((1,), (1,)), ((), ())), preferred_element_type=f32)


def _dot_tn(a, b):
    return lax.dot_general(a, b, (((0,), (0,)), ((), ())), preferred_element_type=f32)


def _split3(x):
    a = x.astype(bf16)
    r = x - a.astype(f32)
    b = r.astype(bf16)
    c = (r - b.astype(f32)).astype(bf16)
    return a, b, c


def _mod_rows(is_sample, p_ref, s_ref, rows):
    p = p_ref[0]
    s = s_ref[...]
    if rows != DEC_BATCH:
        s = jnp.broadcast_to(s[None], (rows // DEC_BATCH, DEC_BATCH, D)).reshape(rows, D)
    return jnp.where(is_sample, s, p)


def _mod_specs(chunk, tiles_per_seq, n_prompt_tiles, extra_args=0):
    def pmap(i, *_):
        return (DEC_BATCH + jnp.minimum(i // tiles_per_seq, BATCH - 1), 0, chunk)

    def smap(i, *_):
        return (0, chunk)

    return (pl.BlockSpec((1, 1, D), pmap), pl.BlockSpec((DEC_BATCH, D), smap))


def _ada_kernel(c_ref, w_ref, b_ref, o_ref):
    c = c_ref[...]
    a = _silu(c).astype(bf16)
    o_ref[0] = _dot(a, w_ref[0].astype(bf16)) + b_ref[0]


def _ada_mod(c_all, w_ada, b_ada):
    rows = c_all.shape[0]
    tn = 1024
    return pl.pallas_call(
        _ada_kernel,
        grid=(DEPTH, 6 * D // tn),
        in_specs=[pl.BlockSpec((rows, D), lambda l, j: (0, 0)),
                  pl.BlockSpec((1, D, tn), lambda l, j: (l, 0, j)),
                  pl.BlockSpec((1, 1, tn), lambda l, j: (l, 0, j))],
        out_specs=pl.BlockSpec((1, rows, tn), lambda l, j: (l, 0, j)),
        out_shape=jax.ShapeDtypeStruct((DEPTH, rows, 6 * D), f32),
        compiler_params=_cparams(("arbitrary", "arbitrary"), 40),
        name="ada_mod",
    )(c_all, w_ada, b_ada.reshape(DEPTH, 1, 6 * D))


def _bucket_np(dist):
    n = np.maximum(dist, 0)
    exact = NUM_BUCKETS // 2
    nf = np.maximum(n, 1).astype(np.float32)
    large = exact + (np.log(nf / np.float32(exact)) / np.float32(math.log(MAX_DISTANCE / exact))
                     * np.float32(NUM_BUCKETS - exact)).astype(np.int32)
    return np.where(n < exact, n, np.clip(large, exact, NUM_BUCKETS - 1)).astype(np.int32)


def _prompt_buckets():
    j = np.arange(2 * WINDOW)
    dist = np.arange(WINDOW)[:, None] + WINDOW - j[None, :]
    band = (dist >= 0) & (dist < WINDOW)
    return np.where(band, _bucket_np(dist), -1).astype(np.int32)


def _sample_buckets():
    l = np.arange(SPAD)[:, None]
    j = np.arange(2 * WINDOW)[None, :]
    dist = WINDOW + l - j
    ok = (dist >= 0) & (dist < WINDOW) & (l < DEC_SEQ) & (j < WINDOW + DEC_SEQ)
    return np.where(ok, _bucket_np(dist), -1).astype(np.int32)


def _bias_kernel(tab_ref, bk_ref, o_ref):
    h = pl.program_id(0)
    bk = bk_ref[...]
    acc = jnp.zeros(bk.shape, f32)
    for i in range(NUM_BUCKETS):
        acc = jnp.where(bk == i, tab_ref[i, h], acc)
    o_ref[0] = jnp.where(bk < 0, NEG, acc)


def _bias_table(rel_bias, buckets):
    r, c = buckets.shape
    return pl.pallas_call(
        _bias_kernel,
        grid=(NH,),
        in_specs=[pl.BlockSpec(memory_space=pltpu.SMEM),
                  pl.BlockSpec((r, c), lambda h: (0, 0))],
        out_specs=pl.BlockSpec((1, r, c), lambda h: (h, 0, 0)),
        out_shape=jax.ShapeDtypeStruct((NH, r, c), f32),
        compiler_params=_cparams(("arbitrary",), 16),
        name="bias_table",
    )(rel_bias, jnp.asarray(buckets))


def _inproj_kernel(x_ref, g_ref, scp_ref, scs_ref, shp_ref, shs_ref, w_ref, o_ref, h_ref):
    i = pl.program_id(0)

    @pl.when(pl.program_id(1) == 0)
    def _():
        y = _rms(x_ref[...], g_ref[...])
        sc = _mod_rows(i >= NPT, scp_ref, scs_ref, TM)
        sh = _mod_rows(i >= NPT, shp_ref, shs_ref, TM)
        h_ref[...] = (y * (1.0 + sc) + sh).astype(bf16)

    o_ref[...] = _dot(h_ref[...], w_ref[0])


def _in_proj(l, x, norm_g, mod3, mod2, w_in_bf):
    scp, scs = _mod_specs(1, TPB, NPT)
    shp, shs = _mod_specs(0, TPB, NPT)
    return pl.pallas_call(
        _inproj_kernel,
        grid=(NT, IN_WIDTH // TN_IN),
        in_specs=[pl.BlockSpec((TM, D), lambda i, j: (i, 0)),
                  pl.BlockSpec((1, D), lambda i, j: (0, 0)),
                  scp, scs, shp, shs,
                  pl.BlockSpec((1, D, TN_IN), lambda i, j: (l, 0, j))],
        out_specs=pl.BlockSpec((TM, TN_IN), lambda i, j: (i, j)),
        out_shape=jax.ShapeDtypeStruct((T, IN_WIDTH), f32),
        scratch_shapes=[pltpu.VMEM((TM, D), bf16)],
        compiler_params=_cparams(("arbitrary", "arbitrary"), 56),
        name="in_proj",
    )(x, norm_g.reshape(1, D), mod3, mod2, mod3, mod2, w_in_bf)


def _hgrn_gates(z, lb):
    f = lb + (1.0 - lb) * jax.nn.sigmoid(z)
    g = jnp.log(jnp.maximum(f, TINY))
    k = (1.0 - lb) * jax.nn.sigmoid(-z)
    return g, k


def _bcast_block_row(b, period, row):
    c, w = b.shape
    b3 = b.reshape(c // period, period, w)
    return jnp.broadcast_to(b3[:, row:row + 1, :], b3.shape).reshape(c, w)


def _hgrn_level_table():
    t = np.arange(HC)[:, None]
    s = np.arange(HC)[None, :]
    lev = np.full((HC, HC), -1, np.int32)
    for li, w in enumerate(HGRN_WIDTHS + (2,)):
        m = (t // (2 * w) == s // (2 * w)) & ((t // w) % 2 == 1) & ((s // w) % 2 == 0)
        lev[m] = li
    lev[(t // 2 == s // 2) & (s <= t)] = len(HGRN_WIDTHS) + 1
    return lev


def _hgrn_prompt_kernel(q_ref, f_ref, v_ref, og_ref, lb_ref, ng_ref, lev_ref, o_ref, sfin_ref, st_ref):
    c = pl.program_id(1)

    @pl.when(c == 0)
    def _():
        st_ref[...] = jnp.zeros_like(st_ref)

    lb = lb_ref[...]
    g, k = _hgrn_gates(f_ref[...], lb)
    qh = _silu(q_ref[...])
    vb = v_ref[...].astype(bf16)
    r = lax.broadcasted_iota(i32, (HC, HC), 0)
    s = lax.broadcasted_iota(i32, (HC, HC), 1)
    tri = (r >= s).astype(bf16)
    g1, g2, g3 = _split3(g)
    b = _dot(tri, g1) + _dot(tri, g2) + _dot(tri, g3)
    lev = lev_ref[...]

    qs, ks = [], []
    for w in HGRN_WIDTHS:
        m = _bcast_block_row(b, 2 * w, w - 1)
        qs.append((qh * jnp.exp(jnp.minimum(b - m, 0.0))).astype(bf16))
        ks.append((k * jnp.exp(jnp.minimum(m - b, 0.0))).astype(bf16))
    row = lax.broadcasted_iota(i32, (HC, 1), 0)
    g_prev = pltpu.roll(g, 1, 0)
    g_next = pltpu.roll(g, HC - 1, 0)
    r4 = row & 3
    qs.append((qh * jnp.exp(jnp.where(r4 == 2, g, jnp.where(r4 == 3, g + g_prev, 0.0)))).astype(bf16))
    ks.append((k * jnp.exp(jnp.where(r4 == 0, g_next, 0.0))).astype(bf16))
    odd = (row & 1) == 1
    qs.append((qh * jnp.exp(jnp.where(odd, g, 0.0))).astype(bf16))
    ks.append((k * jnp.exp(jnp.where(odd, -g, 0.0))).astype(bf16))
    b_end = b[HC - 1:HC, :]
    q_in = (qh * jnp.exp(b)).astype(bf16)
    k_out = (k * jnp.exp(b_end - b)).astype(bf16)
    e_end = jnp.exp(b_end)
    gate = _silu(og_ref[...])
    ng = ng_ref[...]

    for h in range(HH):
        sl = slice(h * DK, (h + 1) * DK)
        a = jnp.zeros((HC, HC), f32)
        for li in range(len(qs)):
            a = jnp.where(lev == li, _dot_nt(qs[li][:, sl], ks[li][:, sl]), a)
        st = st_ref[h]
        o = _dot(a.astype(bf16), vb[:, sl]) + _dot_nt(q_in[:, sl], st.astype(bf16))
        st_new = st * e_end[:, sl] + _dot_tn(vb[:, sl], k_out[:, sl])
        st_ref[h] = st_new
        o_ref[:, sl] = (_rms(o, ng) * gate[:, sl]).astype(bf16)

        @pl.when(c == pl.num_programs(1) - 1)
        def _():
            sfin_ref[0, h] = st_new.T


def _hgrn_prompt(proj, lb, onorm_g):
    nc = SEQ // HC

    def slab(k):
        return pl.BlockSpec((HC, KW), lambda b, c: (b * nc + c, k))

    return pl.pallas_call(
        _hgrn_prompt_kernel,
        grid=(BATCH, nc),
        in_specs=[slab(OFF_HQ // KW), slab(OFF_HF // KW), slab(OFF_HV // KW), slab(OFF_HG // KW),
                  pl.BlockSpec((1, KW), lambda b, c: (0, 0)),
                  pl.BlockSpec((1, DV), lambda b, c: (0, 0)),
                  pl.BlockSpec((HC, HC), lambda b, c: (0, 0))],
        out_specs=[pl.BlockSpec((HC, KW), lambda b, c: (b * nc + c, 0)),
                   pl.BlockSpec((1, HH, DK, DV), lambda b, c: (b, 0, 0, 0))],
        out_shape=[jax.ShapeDtypeStruct((TP, KW), bf16),
                   jax.ShapeDtypeStruct((BATCH, HH, DK, DV), f32)],
        scratch_shapes=[pltpu.VMEM((HH, DV, DK), f32)],
        compiler_params=_cparams(("arbitrary", "arbitrary"), 48),
        name="hgrn_prompt",
    )(proj, proj, proj, proj, lb.reshape(1, KW), onorm_g.reshape(1, DV), jnp.asarray(_hgrn_level_table()))


def _hgrn_sample_kernel(q_ref, f_ref, v_ref, og_ref, lb_ref, ng_ref, s0_ref, o_ref, snew_ref,
                        qin_ref, kout_ref, eend_ref, oacc_ref, mt_ref, q16_ref, ostage_ref):
    step = pl.program_id(0)
    L, B = DEC_SEQ, DEC_BATCH

    @pl.when(step == 0)
    def _():
        lb = lb_ref[...]
        g, k = _hgrn_gates(f_ref[...], lb)
        qh = _silu(q_ref[...])
        v = v_ref[...]
        bs = []
        acc = None
        for t in range(L):
            gt = g[t * B:(t + 1) * B]
            acc = gt if acc is None else acc + gt
            bs.append(acc)
        b_end = bs[-1]
        eend_ref[...] = jnp.exp(b_end)
        for t in range(L):
            rt = slice(t * B, (t + 1) * B)
            qin_ref[rt, :] = qh[rt] * jnp.exp(bs[t])
            kout_ref[rt, :] = k[rt] * jnp.exp(b_end - bs[t])
            for h in range(HH):
                sl = slice(h * DK, (h + 1) * DK)
                o = jnp.zeros((B, DV), f32)
                for s_ in range(t + 1):
                    rs = slice(s_ * B, (s_ + 1) * B)
                    w = jnp.sum(qh[rt, sl] * k[rs, sl] * jnp.exp(bs[t][:, sl] - bs[s_][:, sl]),
                                axis=-1, keepdims=True)
                    o = o + w * v[rs, sl]
                oacc_ref[rt, sl] = o
        mt_ref[...] = jnp.zeros_like(mt_ref)
        q16_ref[...] = jnp.zeros_like(q16_ref)

    rows = [pl.ds(pl.multiple_of(t * B + step * SBB, SBB), SBB) for t in range(L)]
    q_t = [qin_ref[rows[t], :] for t in range(L)]
    k_t = [kout_ref[rows[t], :] for t in range(L)]
    v_t = [v_ref[rows[t], :] for t in range(L)]
    e_t = eend_ref[pl.ds(pl.multiple_of(step * SBB, SBB), SBB), :]
    for bi in range(SBB):
        for h in range(HH):
            sl = slice(h * DK, (h + 1) * DK)
            s0 = s0_ref[0, bi, h]
            for t in range(L):
                q16_ref[t:t + 1, :] = q_t[t][bi:bi + 1, sl]
            oi = _dot(q16_ref[...].astype(bf16), s0.astype(bf16))
            for t in range(L):
                ostage_ref[t, bi:bi + 1, sl] = oi[t:t + 1]
            mt_ref[0:1, :] = e_t[bi:bi + 1, sl]
            for t in range(L):
                mt_ref[t + 1:t + 2, :] = k_t[t][bi:bi + 1, sl]
            cols = mt_ref[...].T
            sn = s0 * cols[:, 0:1]
            for t in range(L):
                sn = sn + cols[:, t + 1:t + 2] * v_t[t][bi:bi + 1, sl]
            snew_ref[bi, h] = sn
    for t in range(L):
        oacc_ref[rows[t], :] += ostage_ref[t]

    @pl.when(step == pl.num_programs(0) - 1)
    def _():
        gate = _silu(og_ref[...])
        ng = ng_ref[...]
        for h in range(HH):
            sl = slice(h * DK, (h + 1) * DK)
            o_ref[:, sl] = (_rms(oacc_ref[:, sl], ng) * gate[:, sl]).astype(bf16)


def _hgrn_sample(l, proj, lb, onorm_g, state):
    def slab(k):
        return pl.BlockSpec((TS, KW), lambda s: (TP // TS, k))

    return pl.pallas_call(
        _hgrn_sample_kernel,
        grid=(DEC_BATCH // SBB,),
        in_specs=[slab(OFF_HQ // KW), slab(OFF_HF // KW), slab(OFF_HV // KW), slab(OFF_HG // KW),
                  pl.BlockSpec((1, KW), lambda s: (0, 0)),
                  pl.BlockSpec((1, DV), lambda s: (0, 0)),
                  pl.BlockSpec((1, SBB, HH, DK, DV), lambda s: (l, s, 0, 0, 0))],
        out_specs=[pl.BlockSpec((TS, KW), lambda s: (0, 0)),
                   pl.BlockSpec((SBB, HH, DK, DV), lambda s: (s, 0, 0, 0))],
        out_shape=[jax.ShapeDtypeStruct((TS, KW), bf16),
                   jax.ShapeDtypeStruct((DEC_BATCH, HH, DK, DV), f32)],
        scratch_shapes=[pltpu.VMEM((TS, KW), f32), pltpu.VMEM((TS, KW), f32),
                        pltpu.VMEM((DEC_BATCH, KW), f32), pltpu.VMEM((TS, KW), f32),
                        pltpu.VMEM((LANES, DK), f32), pltpu.VMEM((2 * SUBLANES, DK), f32),
                        pltpu.VMEM((DEC_SEQ, SBB, KW), f32)],
        compiler_params=_cparams(("arbitrary",), 56),
        name="hgrn_sample",
    )(proj, proj, proj, proj, lb.reshape(1, KW), onorm_g.reshape(1, DV), state)


def _head_rms(x, g, n):
    return jnp.concatenate([_rms(x[:, i * HD:(i + 1) * HD], g) for i in range(n)], axis=1)


def _sink_softmax_pv(s, sink_col, vb):
    m = jnp.maximum(jnp.max(s, axis=-1, keepdims=True), sink_col)
    p = jnp.exp(s - m)
    den = jnp.sum(p, axis=-1, keepdims=True) + jnp.exp(sink_col - m)
    return _dot(p.astype(bf16), vb) / den


def _attn_prompt_kernel(sink_ref, q_ref, kc_ref, kp_ref, vc_ref, vp_ref, qg_ref, kg_ref, bias_ref,
                        o_ref, kn_ref):
    i = pl.program_id(1)
    kc = _head_rms(kc_ref[...], kg_ref[...], NKV)
    kp = _head_rms(kp_ref[...], kg_ref[...], NKV)
    kn_ref[...] = kc
    kk = jnp.concatenate([kp, kc], axis=0).astype(bf16)
    vv = jnp.concatenate([vp_ref[...], vc_ref[...]], axis=0).astype(bf16)
    q = q_ref[...]
    col = lax.broadcasted_iota(i32, (GQ * WINDOW, 2 * WINDOW), 1)
    valid = (i > 0) | (col >= WINDOW)
    for n in range(NKV):
        qn = jnp.concatenate(
            [_rms(q[:, (n * GQ + g) * HD:(n * GQ + g + 1) * HD], qg_ref[...]) for g in range(GQ)], axis=0)
        s = _dot_nt(qn.astype(bf16), kk[:, n * HD:(n + 1) * HD]) * (HD ** -0.5)
        s = s + bias_ref[n * GQ:(n + 1) * GQ].reshape(GQ * WINDOW, 2 * WINDOW)
        s = jnp.where(valid, s, NEG)
        sink_col = jnp.concatenate(
            [jnp.full((WINDOW, 1), sink_ref[n * GQ + g], f32) for g in range(GQ)], axis=0)
        o = _sink_softmax_pv(s, sink_col, vv[:, n * HD:(n + 1) * HD])
        for g in range(GQ):
            hh = n * GQ + g
            o_ref[:, hh * HD:(hh + 1) * HD] = o[g * WINDOW:(g + 1) * WINDOW].astype(bf16)


def _attn_prompt(proj, sinks, q_norm_g, k_norm_g, bias):
    nb = SEQ // WINDOW
    kblk, vblk = OFF_AK // KVW, OFF_AV // KVW

    def cur(col):
        return lambda b, i: (b * nb + i, col)

    def prev(col):
        return lambda b, i: (b * nb + jnp.maximum(i - 1, 0), col)

    return pl.pallas_call(
        _attn_prompt_kernel,
        grid=(BATCH, nb),
        in_specs=[pl.BlockSpec(memory_space=pltpu.SMEM),
                  pl.BlockSpec((WINDOW, NH * HD), cur(OFF_AQ // (NH * HD))),
                  pl.BlockSpec((WINDOW, KVW), cur(kblk)),
                  pl.BlockSpec((WINDOW, KVW), prev(kblk)),
                  pl.BlockSpec((WINDOW, KVW), cur(vblk)),
                  pl.BlockSpec((WINDOW, KVW), prev(vblk)),
                  pl.BlockSpec((1, HD), lambda b, i: (0, 0)),
                  pl.BlockSpec((1, HD), lambda b, i: (0, 0)),
                  pl.BlockSpec((NH, WINDOW, 2 * WINDOW), lambda b, i: (0, 0, 0))],
        out_specs=[pl.BlockSpec((WINDOW, NH * HD), lambda b, i: (b * nb + i, 0)),
                   pl.BlockSpec((WINDOW, KVW), lambda b, i: (b * nb + i, 0))],
        out_shape=[jax.ShapeDtypeStruct((TP, NH * HD), bf16),
                   jax.ShapeDtypeStruct((TP, KVW), f32)],
        compiler_params=_cparams(("arbitrary", "arbitrary"), 32),
        name="attn_prompt",
    )(sinks, proj, proj, proj, proj, proj, q_norm_g.reshape(1, HD), k_norm_g.reshape(1, HD), bias)


def _attn_sample_kernel(sink_ref, q_ref, k_ref, v_ref, ck_ref, cv_ref, qg_ref, kg_ref, bias_ref,
                        o_ref, kw_ref, vw_ref):
    pad_rows = 2 * WINDOW - WINDOW - SPAD
    for bi in range(SBB):
        kn = _head_rms(k_ref[:, bi, :], kg_ref[...], NKV)
        vn = v_ref[:, bi, :]
        ck = ck_ref[0, bi]
        cv = cv_ref[0, bi]
        kw_ref[bi, 0:WINDOW - DEC_SEQ, :] = ck[DEC_SEQ:, :]
        kw_ref[bi, WINDOW - DEC_SEQ:WINDOW, :] = kn[0:DEC_SEQ]
        vw_ref[bi, 0:WINDOW - DEC_SEQ, :] = cv[DEC_SEQ:, :]
        vw_ref[bi, WINDOW - DEC_SEQ:WINDOW, :] = vn[0:DEC_SEQ]
        zpad = jnp.zeros((pad_rows, KVW), f32)
        kk = jnp.concatenate([ck, kn, zpad], axis=0).astype(bf16)
        vv = jnp.concatenate([cv, vn, zpad], axis=0).astype(bf16)
        q = q_ref[:, bi, :]
        for n in range(NKV):
            qn = jnp.concatenate(
                [_rms(q[:, (n * GQ + g) * HD:(n * GQ + g + 1) * HD], qg_ref[...]) for g in range(GQ)], axis=0)
            s = _dot_nt(qn.astype(bf16), kk[:, n * HD:(n + 1) * HD]) * (HD ** -0.5)
            s = s + bias_ref[n * GQ:(n + 1) * GQ].reshape(GQ * SPAD, 2 * WINDOW)
            sink_col = jnp.concatenate(
                [jnp.full((SPAD, 1), sink_ref[n * GQ + g], f32) for g in range(GQ)], axis=0)
            o = _sink_softmax_pv(s, sink_col, vv[:, n * HD:(n + 1) * HD])
            for g in range(GQ):
                hh = n * GQ + g
                o_ref[:, bi, hh * HD:(hh + 1) * HD] = o[g * SPAD:(g + 1) * SPAD].astype(bf16)


def _attn_sample(l, qkv_pad, cache_k, cache_v, sinks, q_norm_g, k_norm_g, bias):
    kblk = NH * HD // KVW
    return pl.pallas_call(
        _attn_sample_kernel,
        grid=(DEC_BATCH // SBB,),
        in_specs=[pl.BlockSpec(memory_space=pltpu.SMEM),
                  pl.BlockSpec((SPAD, SBB, NH * HD), lambda s: (0, s, 0)),
                  pl.BlockSpec((SPAD, SBB, KVW), lambda s: (0, s, kblk)),
                  pl.BlockSpec((SPAD, SBB, KVW), lambda s: (0, s, kblk + 1)),
                  pl.BlockSpec((1, SBB, WINDOW, KVW), lambda s: (l, s, 0, 0)),
                  pl.BlockSpec((1, SBB, WINDOW, KVW), lambda s: (l, s, 0, 0)),
                  pl.BlockSpec((1, HD), lambda s: (0, 0)),
                  pl.BlockSpec((1, HD), lambda s: (0, 0)),
                  pl.BlockSpec((NH, SPAD, 2 * WINDOW), lambda s: (0, 0, 0))],
        out_specs=[pl.BlockSpec((SPAD, SBB, NH * HD), lambda s: (0, s, 0)),
                   pl.BlockSpec((SBB, WINDOW, KVW), lambda s: (s, 0, 0)),
                   pl.BlockSpec((SBB, WINDOW, KVW), lambda s: (s, 0, 0))],
        out_shape=[jax.ShapeDtypeStruct((SPAD, DEC_BATCH, NH * HD), bf16),
                   jax.ShapeDtypeStruct((DEC_BATCH, WINDOW, KVW), f32),
                   jax.ShapeDtypeStruct((DEC_BATCH, WINDOW, KVW), f32)],
        compiler_params=_cparams(("arbitrary",), 32),
        name="attn_sample",
    )(sinks, qkv_pad, qkv_pad, qkv_pad, cache_k, cache_v, q_norm_g.reshape(1, HD), k_norm_g.reshape(1, HD), bias)


def _merge_kernel(oh_ref, oa_ref, ga_ref, gb_ref, x_ref, g1p_ref, g1s_ref, wa_ref, wb_ref, wo_ref,
                  o_ref, acc_ref):
    i, j = pl.program_id(0), pl.program_id(1)
    merged = (jax.nn.sigmoid(ga_ref[...]) * _dot(oh_ref[...], wa_ref[0])
              + jax.nn.sigmoid(gb_ref[...]) * _dot(oa_ref[...], wb_ref[0]))
    part = _dot(merged.astype(bf16), wo_ref[0])

    @pl.when(j == 0)
    def _():
        acc_ref[...] = part

    @pl.when(j > 0)
    def _():
        acc_ref[...] += part

    @pl.when(j == pl.num_programs(1) - 1)
    def _():
        g1 = _mod_rows(i >= NPT, g1p_ref, g1s_ref, TM)
        o_ref[...] = x_ref[...] + g1 * acc_ref[...]


def _merge(l, o_h, o_a, proj, x, mod3, mod2, wa_bf, wb_bf, wo_bf):
    g1p, g1s = _mod_specs(2, TPB, NPT)
    return pl.pallas_call(
        _merge_kernel,
        grid=(NT, D // TN_MG),
        in_specs=[pl.BlockSpec((TM, KW), lambda i, j: (i, 0)),
                  pl.BlockSpec((TM, NH * HD), lambda i, j: (i, 0)),
                  pl.BlockSpec((TM, TN_MG), lambda i, j: (i, OFF_GA // TN_MG + j)),
                  pl.BlockSpec((TM, TN_MG), lambda i, j: (i, OFF_GB // TN_MG + j)),
                  pl.BlockSpec((TM, D), lambda i, j: (i, 0)),
                  g1p, g1s,
                  pl.BlockSpec((1, KW, TN_MG), lambda i, j: (l, 0, j)),
                  pl.BlockSpec((1, NH * HD, TN_MG), lambda i, j: (l, 0, j)),
                  pl.BlockSpec((1, TN_MG, D), lambda i, j: (l, j, 0))],
        out_specs=pl.BlockSpec((TM, D), lambda i, j: (i, 0)),
        out_shape=jax.ShapeDtypeStruct((T, D), f32),
        scratch_shapes=[pltpu.VMEM((TM, D), f32)],
        compiler_params=_cparams(("arbitrary", "arbitrary"), 48),
        name="merge",
    )(o_h, o_a, proj, proj, x, mod3, mod2, wa_bf, wb_bf, wo_bf)


def _first_index(hit, idx, big, axis):
    return jnp.min(jnp.where(hit, idx, big), axis=axis, keepdims=True)


def _pack_bf16_pairs(h):
    lo = lax.bitcast_convert_type(h[:, :D // 2].astype(bf16).astype(f32), u32)
    hi = lax.bitcast_convert_type(h[:, D // 2:].astype(bf16).astype(f32), u32)
    return (hi & jnp.uint32(0xFFFF0000)) | (lo >> 16)


def _unpack_bf16_pairs(p):
    lo = lax.bitcast_convert_type(p << 16, f32).astype(bf16)
    hi = lax.bitcast_convert_type(p & jnp.uint32(0xFFFF0000), f32).astype(bf16)
    return lo, hi


def _router_kernel(x_ref, g_ref, scp_ref, scs_ref, shp_ref, shs_ref, wr_ref, rb_ref,
                   hp_ref, ids_ref, wts_ref, rank_ref, cnt_ref):
    i = pl.program_id(0)
    y = _rms(x_ref[...], g_ref[...])
    sc = _mod_rows(i >= NPT, scp_ref, scs_ref, TM)
    sh = _mod_rows(i >= NPT, shp_ref, shs_ref, TM)
    h = y * (1.0 + sc) + sh
    hp_ref[...] = _pack_bf16_pairs(h).reshape(TM, ROW_TILES_X, LANES)
    wr = wr_ref[0]
    w1 = wr.astype(bf16)
    w2 = (wr - w1.astype(f32)).astype(bf16)
    h1 = h.astype(bf16)
    h2 = (h - h1.astype(f32)).astype(bf16)
    logits = _dot_nt(w1, h1) + (_dot_nt(w1, h2) + _dot_nt(w2, h1))
    scores = jax.nn.sigmoid(logits)
    biased = scores + rb_ref[...]
    b3 = biased.reshape(NG, EPG, TM)
    e_in = lax.broadcasted_iota(i32, (NG, EPG, TM), 1).astype(f32)
    m1 = jnp.max(b3, axis=1, keepdims=True)
    first = _first_index(b3 == m1, e_in, float(EPG), 1)
    m2 = jnp.max(jnp.where(e_in == first, -jnp.inf, b3), axis=1, keepdims=True)
    gs = (m1 + m2).reshape(NG, TM)
    g_idx = lax.broadcasted_iota(i32, (NG, TM), 0).astype(f32)
    gsel = jnp.zeros((NG, TM), f32)
    for _ in range(TOPKG):
        m = jnp.max(gs, axis=0, keepdims=True)
        pick = g_idx == _first_index(gs == m, g_idx, float(NG), 0)
        gsel = jnp.where(pick, 1.0, gsel)
        gs = jnp.where(pick, -jnp.inf, gs)
    emask = jnp.broadcast_to(gsel.reshape(NG, 1, TM), (NG, EPG, TM)).reshape(NE, TM)
    cur = jnp.where(emask > 0.5, biased, NEG)
    e_idx = lax.broadcasted_iota(i32, (NE, TM), 0).astype(f32)
    ids, ws, picks = [], [], []
    for _ in range(TOPK):
        m = jnp.max(cur, axis=0, keepdims=True)
        first = _first_index(cur == m, e_idx, float(NE), 0)
        pick = e_idx == first
        ids.append(first)
        picks.append(pick)
        ws.append(jnp.sum(jnp.where(pick, scores, 0.0), axis=0, keepdims=True))
        cur = jnp.where(pick, -jnp.inf, cur)
    w = jnp.concatenate(ws, axis=0)
    ids_ref[...] = jnp.concatenate(ids, axis=0).astype(i32)
    wts_ref[...] = w / jnp.sum(w, axis=0, keepdims=True) * ROUTED_SCALE

    @pl.when(i == 0)
    def _():
        cnt_ref[...] = jnp.zeros_like(cnt_ref)

    onehot = jnp.zeros((NE, TM), f32)
    for pick in picks:
        onehot = jnp.where(pick, 1.0, onehot)
    before = (lax.broadcasted_iota(i32, (TM, TM), 0) < lax.broadcasted_iota(i32, (TM, TM), 1)).astype(bf16)
    base = cnt_ref[:, 0:1] + _dot(onehot.astype(bf16), before)
    rank_ref[...] = jnp.concatenate(
        [jnp.sum(jnp.where(pick, base, 0.0), axis=0, keepdims=True) for pick in picks], axis=0).astype(i32)
    cnt_ref[...] += jnp.sum(onehot, axis=1, keepdims=True)


def _router(l, x1, norm_g, mod3, mod2, w_router_t, router_bias):
    scp, scs = _mod_specs(4, TPB, NPT)
    shp, shs = _mod_specs(3, TPB, NPT)
    return pl.pallas_call(
        _router_kernel,
        grid=(NT,),
        in_specs=[pl.BlockSpec((TM, D), lambda i: (i, 0)),
                  pl.BlockSpec((1, D), lambda i: (0, 0)),
                  scp, scs, shp, shs,
                  pl.BlockSpec((1, NE, D), lambda i: (l, 0, 0)),
                  pl.BlockSpec((NE, 1), lambda i: (0, 0))],
        out_specs=[pl.BlockSpec((TM, ROW_TILES_X, LANES), lambda i: (i, 0, 0)),
                   pl.BlockSpec((TOPK, TM), lambda i: (0, i)),
                   pl.BlockSpec((TOPK, TM), lambda i: (0, i)),
                   pl.BlockSpec((TOPK, TM), lambda i: (0, i)),
                   pl.BlockSpec((NE, LANES), lambda i: (0, 0))],
        out_shape=[jax.ShapeDtypeStruct((T, ROW_TILES_X, LANES), u32),
                   jax.ShapeDtypeStruct((TOPK, T), i32),
                   jax.ShapeDtypeStruct((TOPK, T), f32),
                   jax.ShapeDtypeStruct((TOPK, T), i32),
                   jax.ShapeDtypeStruct((NE, LANES), f32)],
        compiler_params=_cparams(("arbitrary",), 48),
        name="router",
    )(x1, norm_g.reshape(1, D), mod3, mod2, mod3, mod2, w_router_t, router_bias.reshape(NE, 1))


def _block_tables(cnt):
    counts = cnt[:, 0].astype(i32)
    padded = (counts + BLK - 1) // BLK * BLK
    pend = jnp.cumsum(padded)
    pstart = pend - padded
    blk_start = jnp.arange(NBLK, dtype=i32) * BLK
    block_e = jnp.minimum(jnp.sum(blk_start[:, None] >= pend[None, :], axis=1), NE - 1).astype(i32)
    n_used = (pend[-1] // BLK).astype(i32).reshape(1)
    after = (pend // BLK).astype(i32)
    next_of_expert = jnp.where(after < n_used[0], block_e[jnp.minimum(after, NBLK - 1)], -1)
    next_e = next_of_expert[block_e].astype(i32)
    return (block_e, n_used, next_e, pstart.astype(i32), (pstart + counts).astype(i32),
            (padded - counts).astype(i32))


def _pos_kernel(ps_ref, ids_ref, rank_ref, pos_ref):
    ids = ids_ref[...]
    acc = rank_ref[...]
    for e in range(NE):
        acc = acc + jnp.where(ids == e, ps_ref[e], 0)
    pos_ref[...] = acc


def _slot_positions(pstart, ids_t, rank_t):
    return pl.pallas_call(
        _pos_kernel,
        grid=(1,),
        in_specs=[pl.BlockSpec(memory_space=pltpu.SMEM),
                  pl.BlockSpec((TOPK, T), lambda i: (0, 0)),
                  pl.BlockSpec((TOPK, T), lambda i: (0, 0))],
        out_specs=pl.BlockSpec((TOPK, T), lambda i: (0, 0)),
        out_shape=jax.ShapeDtypeStruct((TOPK, T), i32),
        compiler_params=_cparams(("arbitrary",), 16),
        name="slot_positions",
    )(pstart, ids_t, rank_t)


def _dispatch_kernel(pad0_ref, padn_ref, nu_ref, pos_ref, hp_hbm, xs_hbm, zero_ref, sem):
    i = pl.program_id(0)

    def tok_copy(k, j, u):
        r = j * SUBLANES + u
        return pltpu.make_async_copy(hp_hbm.at[i * TMF + r], xs_hbm.at[pos_ref[0, 0, k * TMF + r]], sem)

    def pad_copy(e, r):
        return pltpu.make_async_copy(zero_ref.at[0], xs_hbm.at[pad0_ref[e] + r], sem)

    def tail_copy(b):
        return pltpu.make_async_copy(zero_ref, xs_hbm.at[pl.ds(b * BLK, BLK)], sem)

    def for_tokens(fn):
        for k in range(TOPK):
            def body(j, c):
                for u in range(SUBLANES):
                    fn(tok_copy(k, j, u), u)
                return c
            lax.fori_loop(0, TMF // SUBLANES, body, 0)

    def for_pads(fn):
        def per_expert(e, c):
            def body(r, c2):
                fn(pad_copy(e, r), 0)
                return c2
            return lax.fori_loop(0, padn_ref[e], body, c)
        lax.fori_loop(0, NE, per_expert, 0)

        def per_block(b, c):
            fn(tail_copy(b), 0)
            return c
        lax.fori_loop(nu_ref[0], NBLK, per_block, 0)

    @pl.when(i == 0)
    def _():
        zero_ref[...] = jnp.zeros_like(zero_ref)
        for_pads(lambda cp, u: cp.start(priority=u % 2))

    for_tokens(lambda cp, u: cp.start(priority=u % 2))
    for_tokens(lambda cp, u: cp.wait())

    @pl.when(i == 0)
    def _():
        for_pads(lambda cp, u: cp.wait())


def _dispatch(pad_start, pad_cnt, n_used, pos_tiles, hp):
    grid_spec = pltpu.PrefetchScalarGridSpec(
        num_scalar_prefetch=3,
        grid=(T // TMF,),
        in_specs=[pl.BlockSpec((1, 1, TOPK * TMF), lambda i, p0, pn, nu: (i, 0, 0), memory_space=pltpu.SMEM),
                  pl.BlockSpec(memory_space=pl.ANY)],
        out_specs=pl.BlockSpec(memory_space=pl.ANY),
        scratch_shapes=[pltpu.VMEM((BLK, ROW_TILES_X, LANES), u32), pltpu.SemaphoreType.DMA(())],
    )
    return pl.pallas_call(
        _dispatch_kernel,
        grid_spec=grid_spec,
        out_shape=jax.ShapeDtypeStruct((NROW, ROW_TILES_X, LANES), u32),
        compiler_params=_cparams(("arbitrary",), 16),
        name="dispatch",
    )(pad_start, pad_cnt, n_used, pos_tiles, hp)


def _expert_kernel(l, be_ref, nu_ref, nx_ref, x_ref, wg_hbm, wu_hbm, wd_hbm, y_ref,
                   wg_f, wu_f, wd_f, wsem, ord_ref, wgb, wub, wdb):
    i = pl.program_id(0)

    def weight_copies(e, s):
        return (pltpu.make_async_copy(wg_hbm.at[l, e], wg_f.at[s], wsem.at[s]),
                pltpu.make_async_copy(wu_hbm.at[l, e], wu_f.at[s], wsem.at[s]),
                pltpu.make_async_copy(wd_hbm.at[l, e], wd_f.at[s], wsem.at[s]))

    @pl.when(i == 0)
    def _():
        ord_ref[0] = 0
        for cp in weight_copies(be_ref[0], 0):
            cp.start()

    @pl.when(i >= nu_ref[0])
    def _():
        y_ref[...] = jnp.zeros_like(y_ref)

    @pl.when(i < nu_ref[0])
    def _():
        @pl.when((i == 0) | (be_ref[i] != be_ref[jnp.maximum(i - 1, 0)]))
        def _():
            s = ord_ref[0] % 2
            for cp in weight_copies(be_ref[i], s):
                cp.wait()
            wgb[...] = wg_f[s].astype(bf16)
            wub[...] = wu_f[s].astype(bf16)
            wdb[...] = wd_f[s].astype(bf16)

            @pl.when(nx_ref[i] >= 0)
            def _():
                for cp in weight_copies(nx_ref[i], 1 - s):
                    cp.start()

            ord_ref[0] = ord_ref[0] + 1

        lo, hi = _unpack_bf16_pairs(x_ref[...].reshape(BLK, D // 2))
        half = D // 2
        g = _dot(lo, wgb[0:half, :]) + _dot(hi, wgb[half:D, :])
        u = _dot(lo, wub[0:half, :]) + _dot(hi, wub[half:D, :])
        y = _dot((_silu(g) * u).astype(bf16), wdb[...])
        y_ref[...] = _pack_bf16_pairs(y).reshape(BLK, ROW_TILES_X, LANES)


def _experts(l, xs, block_e, n_used, next_e, w_gate, w_up, w_down):
    grid_spec = pltpu.PrefetchScalarGridSpec(
        num_scalar_prefetch=3,
        grid=(NBLK,),
        in_specs=[pl.BlockSpec((BLK, ROW_TILES_X, LANES), lambda i, be, nu, nx: (jnp.minimum(i, nu[0] - 1), 0, 0)),
                  pl.BlockSpec(memory_space=pl.ANY),
                  pl.BlockSpec(memory_space=pl.ANY),
                  pl.BlockSpec(memory_space=pl.ANY)],
        out_specs=pl.BlockSpec((BLK, ROW_TILES_X, LANES), lambda i, be, nu, nx: (i, 0, 0)),
        scratch_shapes=[pltpu.VMEM((2, D, DE), f32), pltpu.VMEM((2, D, DE), f32), pltpu.VMEM((2, DE, D), f32),
                        pltpu.SemaphoreType.DMA((2,)), pltpu.SMEM((1,), i32),
                        pltpu.VMEM((D, DE), bf16), pltpu.VMEM((D, DE), bf16), pltpu.VMEM((DE, D), bf16)],
    )
    return pl.pallas_call(
        functools.partial(_expert_kernel, l),
        grid_spec=grid_spec,
        out_shape=jax.ShapeDtypeStruct((NROW, ROW_TILES_X, LANES), u32),
        compiler_params=_cparams(("arbitrary",), 56),
        name="experts",
    )(block_e, n_used, next_e, xs, w_gate, w_up, w_down)


def _final_kernel(pos_ref, posn_ref, x_ref, hp_ref, w_ref, g2p_ref, g2s_ref, sg_ref, su_ref, sd_ref,
                  y_hbm, o_ref, ybuf, sem):
    i = pl.program_id(0)
    n = pl.num_programs(0)
    slot = i % 2

    def row_copy(pref, k, j, u, s):
        r = j * SUBLANES + u
        return pltpu.make_async_copy(y_hbm.at[pref[0, 0, k * TMF + r]], ybuf.at[s, k, r], sem.at[s])

    def for_rows(pref, s, fn):
        for k in range(TOPK):
            def body(j, c):
                for u in range(SUBLANES):
                    fn(row_copy(pref, k, j, u, s), u)
                return c
            lax.fori_loop(0, TMF // SUBLANES, body, 0)

    @pl.when(i == 0)
    def _():
        for_rows(pos_ref, 0, lambda cp, u: cp.start(priority=u % 2))

    @pl.when(i + 1 < n)
    def _():
        for_rows(posn_ref, 1 - slot, lambda cp, u: cp.start(priority=u % 2))

    lo, hi = _unpack_bf16_pairs(hp_ref[...].reshape(TMF, D // 2))
    hb = jnp.concatenate([lo, hi], axis=1)
    shared = _dot((_silu(_dot(hb, sg_ref[0])) * _dot(hb, su_ref[0])).astype(bf16), sd_ref[0])

    for_rows(pos_ref, slot, lambda cp, u: cp.wait())

    w = w_ref[...]
    r_lo = jnp.zeros((TMF, D // 2), f32)
    r_hi = jnp.zeros((TMF, D // 2), f32)
    for k in range(TOPK):
        p = ybuf[slot, k].reshape(TMF, D // 2)
        r_lo = r_lo + w[:, k:k + 1] * lax.bitcast_convert_type(p << 16, f32)
        r_hi = r_hi + w[:, k:k + 1] * lax.bitcast_convert_type(p & jnp.uint32(0xFFFF0000), f32)
    routed = jnp.concatenate([r_lo, r_hi], axis=1)
    g2 = _mod_rows(i >= TP // TMF, g2p_ref, g2s_ref, TMF)
    o_ref[...] = x_ref[...] + g2 * (shared + routed)


def _final(l, x1, hp, wts, pos_tiles, mod3, mod2, sg_bf, su_bf, sd_bf, yb):
    g2p, g2s = _mod_specs(5, SEQ // TMF, TP // TMF)
    ntile = T // TMF
    return pl.pallas_call(
        _final_kernel,
        grid=(ntile,),
        in_specs=[pl.BlockSpec((1, 1, TOPK * TMF), lambda i: (i, 0, 0), memory_space=pltpu.SMEM),
                  pl.BlockSpec((1, 1, TOPK * TMF), lambda i: (jnp.minimum(i + 1, ntile - 1), 0, 0),
                               memory_space=pltpu.SMEM),
                  pl.BlockSpec((TMF, D), lambda i: (i, 0)),
                  pl.BlockSpec((TMF, ROW_TILES_X, LANES), lambda i: (i, 0, 0)),
                  pl.BlockSpec((TMF, TOPK), lambda i: (i, 0)),
                  g2p, g2s,
                  pl.BlockSpec((1, D, DE), lambda i: (l, 0, 0)),
                  pl.BlockSpec((1, D, DE), lambda i: (l, 0, 0)),
                  pl.BlockSpec((1, DE, D), lambda i: (l, 0, 0)),
                  pl.BlockSpec(memory_space=pl.ANY)],
        out_specs=pl.BlockSpec((TMF, D), lambda i: (i, 0)),
        out_shape=jax.ShapeDtypeStruct((T, D), f32),
        scratch_shapes=[pltpu.VMEM((2, TOPK, TMF, ROW_TILES_X, LANES), u32),
                        pltpu.SemaphoreType.DMA((2,))],
        compiler_params=_cparams(("arbitrary",), 56),
        name="moe_combine",
    )(pos_tiles, pos_tiles, x1, hp, wts, mod3, mod2, sg_bf, su_bf, sd_bf, yb)


def _layer(l, x, mod, lb, state, cache_k, cache_v, bias_p, bias_s, norm1_g, norm2_g, w_in_bf, onorm_g,
           q_norm_g, k_norm_g, sinks, wa_bf, wb_bf, wo_bf, w_router_t, router_bias,
           w_eg, w_eu, w_ed, sg_bf, su_bf, sd_bf):
    mod3 = mod.reshape(mod.shape[0], 1, 6 * D)
    proj = _in_proj(l, x, norm1_g, mod3, mod, w_in_bf)

    oh_p, s_p = _hgrn_prompt(proj, lb, onorm_g)
    oh_s, s_s = _hgrn_sample(l, proj, lb, onorm_g, state)

    oa_p, kn_p = _attn_prompt(proj, sinks, q_norm_g, k_norm_g, bias_p)
    qkv_pad = jnp.pad(proj[TP:, OFF_AQ:OFF_GA].reshape(DEC_SEQ, DEC_BATCH, OFF_GA - OFF_AQ),
                      ((0, SPAD - DEC_SEQ), (0, 0), (0, 0)))
    oa_s, kw_s, vw_s = _attn_sample(l, qkv_pad, cache_k, cache_v, sinks, q_norm_g, k_norm_g, bias_s)

    o_h = jnp.concatenate([oh_p, oh_s], axis=0)
    o_a = jnp.concatenate([oa_p, oa_s[:DEC_SEQ].reshape(TS, NH * HD)], axis=0)
    x1 = _merge(l, o_h, o_a, proj, x, mod3, mod, wa_bf, wb_bf, wo_bf)

    hp, ids_t, wts_t, rank_t, cnt = _router(l, x1, norm2_g, mod3, mod, w_router_t, router_bias)
    block_e, n_used, next_e, pstart, pad_start, pad_cnt = _block_tables(cnt)
    pos = _slot_positions(pstart, ids_t, rank_t)
    pos_tiles = pos.reshape(TOPK, T // TMF, TMF).transpose(1, 0, 2).reshape(T // TMF, 1, TOPK * TMF)
    xs = _dispatch(pad_start, pad_cnt, n_used, pos_tiles, hp)
    yb = _experts(l, xs, block_e, n_used, next_e, w_eg, w_eu, w_ed)
    x2 = _final(l, x1, hp, wts_t.T, pos_tiles, mod3, mod, sg_bf, su_bf, sd_bf, yb)

    k_p = kn_p.reshape(BATCH, SEQ, NKV, HD)[:, SEQ - WINDOW:]
    v_p = proj[:TP, OFF_AV:OFF_GA].reshape(BATCH, SEQ, NKV, HD)[:, SEQ - WINDOW:]
    return (x2, s_p, s_s, k_p, v_p, kw_s.reshape(DEC_BATCH, WINDOW, NKV, HD),
            vw_s.reshape(DEC_BATCH, WINDOW, NKV, HD))


def kernel(x_prompt, x_sample, state_hgrn, cache_k_win, cache_v_win, c_prompt, c_sample, norm1_g, norm2_g,
           w_ada, b_ada, w_in, hgrn_lb_logits, hgrn_onorm_g, q_norm_g, k_norm_g, attn_sinks, rel_bias,
           w_branch_a, w_branch_b, w_out, w_router, router_bias, w_exp_gate, w_exp_up, w_exp_down,
           w_sh_gate, w_sh_up, w_sh_down):
    p_lb = jax.nn.softmax(hgrn_lb_logits.astype(f32), axis=0)
    lower_bounds = jnp.clip(jnp.cumsum(p_lb, axis=0) - p_lb[0], 0.0, 1.0 - 1e-6)

    c_all = jnp.concatenate([c_sample, c_prompt, jnp.zeros((SUBLANES - BATCH, D), f32)], axis=0)
    mod = _ada_mod(c_all, w_ada, b_ada)
    bias_p = _bias_table(rel_bias, _prompt_buckets())
    bias_s = _bias_table(rel_bias, _sample_buckets())

    x = jnp.concatenate([x_prompt.reshape(TP, D), x_sample.transpose(1, 0, 2).reshape(TS, D)], axis=0)
    cache_k = cache_k_win.reshape(DEPTH, DEC_BATCH, WINDOW, KVW)
    cache_v = cache_v_win.reshape(DEPTH, DEC_BATCH, WINDOW, KVW)
    w_in_bf, wa_bf, wb_bf, wo_bf = (w.astype(bf16) for w in (w_in, w_branch_a, w_branch_b, w_out))
    sg_bf, su_bf, sd_bf = (w.astype(bf16) for w in (w_sh_gate, w_sh_up, w_sh_down))
    w_router_t = w_router.transpose(0, 2, 1)
    sp_l, ss_l, kp_l, vp_l, ks_l, vs_l = [], [], [], [], [], []
    for l in range(DEPTH):
        x, s_p, s_s, k_p, v_p, k_s, v_s = _layer(
            l, x, mod[l], lower_bounds[l], state_hgrn, cache_k, cache_v, bias_p, bias_s,
            norm1_g[l], norm2_g[l], w_in_bf, hgrn_onorm_g[l], q_norm_g[l], k_norm_g[l], attn_sinks[l],
            wa_bf, wb_bf, wo_bf, w_router_t, router_bias[l],
            w_exp_gate, w_exp_up, w_exp_down, sg_bf, su_bf, sd_bf)
        sp_l.append(s_p); ss_l.append(s_s); kp_l.append(k_p); vp_l.append(v_p); ks_l.append(k_s); vs_l.append(v_s)

    y_prompt = x[:TP].reshape(BATCH, SEQ, D)
    y_sample = x[TP:].reshape(DEC_SEQ, DEC_BATCH, D).transpose(1, 0, 2)
    return (y_prompt, y_sample, jnp.stack(sp_l), jnp.stack(ss_l), jnp.stack(kp_l), jnp.stack(vp_l),
            jnp.stack(ks_l), jnp.stack(vs_l))
```

```python
import functools
import math

import numpy as np
import jax
import jax.numpy as jnp
from jax import lax
from jax.experimental import pallas as pl
from jax.experimental.pallas import tpu as pltpu

f32 = jnp.float32
bf16 = jnp.bfloat16
i32 = jnp.int32
u32 = jnp.uint32

D = 2048
BATCH, SEQ = 2, 4096
DEPTH = 2
DEC_BATCH, DEC_SEQ = 128, 4
HH, DK, DV = 8, 128, 128
KW = HH * DK
NH, NKV, HD = 16, 4, 64
GQ = NH // NKV
KVW = NKV * HD
WINDOW = 128
NUM_BUCKETS, MAX_DISTANCE = 32, 128
NE, NG, TOPKG, TOPK, DE = 64, 8, 4, 8, 512
EPG = NE // NG
ROUTED_SCALE = 2.5
IN_WIDTH = 2 * KW + 2 * KW + NH * HD + 2 * KVW + 2 * D
EPS = 1e-6
NEG = -1e30
TINY = 1e-30

OFF_HQ, OFF_HF, OFF_HV, OFF_HG = 0, KW, 2 * KW, 3 * KW
OFF_AQ = 4 * KW
OFF_AK = OFF_AQ + NH * HD
OFF_AV = OFF_AK + KVW
OFF_GA = OFF_AV + KVW
OFF_GB = OFF_GA + D

TP = BATCH * SEQ
TS = DEC_BATCH * DEC_SEQ
T = TP + TS
TM = 512
NPT = TP // TM
NT = T // TM
TPB = SEQ // TM
TN_IN = 2432
TN_MG = 512
TMF = 128
HC = 256
HGRN_WIDTHS = (128, 64, 32, 16, 8, 4)
SBB = 8
SPAD = 8
BLK = 256
NA = T * TOPK
NBLK = NA // BLK + NE
NROW = NBLK * BLK

V7X_VMEM_BYTES = 64 * 1024 * 1024
SUBLANES, LANES = 8, 128
ROW_TILES_X = D // 2 // LANES


def _cparams(sem, vmem_mib):
    assert vmem_mib * 1024 * 1024 < V7X_VMEM_BYTES
    return pltpu.CompilerParams(dimension_semantics=sem, vmem_limit_bytes=vmem_mib * 1024 * 1024)


def _silu(x):
    return x * jax.nn.sigmoid(x)


def _rms(x, g):
    return x * lax.rsqrt(jnp.mean(x * x, axis=-1, keepdims=True) + EPS) * g


def _dot(a, b):
    return jnp.dot(a, b, preferred_element_type=f32)


def _dot_nt(a, b):
    return lax.dot_general(a, b, (((1,), (1,)), ((), ())), preferred_element_type=f32)


def _dot_tn(a, b):
    return lax.dot_general(a, b, (((0,), (0,)), ((), ())), preferred_element_type=f32)


def _split3(x):
    a = x.astype(bf16)
    r = x - a.astype(f32)
    b = r.astype(bf16)
    c = (r - b.astype(f32)).astype(bf16)
    return a, b, c


def _mod_rows(is_sample, p_ref, s_ref, rows):
    p = p_ref[0]
    s = s_ref[...]
    if rows != DEC_BATCH:
        s = jnp.broadcast_to(s[None], (rows // DEC_BATCH, DEC_BATCH, D)).reshape(rows, D)
    return jnp.where(is_sample, s, p)


def _mod_specs(chunk, tiles_per_seq, n_prompt_tiles, extra_args=0):
    def pmap(i, *_):
        return (DEC_BATCH + jnp.minimum(i // tiles_per_seq, BATCH - 1), 0, chunk)

    def smap(i, *_):
        return (0, chunk)

    return (pl.BlockSpec((1, 1, D), pmap), pl.BlockSpec((DEC_BATCH, D), smap))


def _ada_kernel(c_ref, w_ref, b_ref, o_ref):
    c = c_ref[...]
    a = _silu(c).astype(bf16)
    o_ref[0] = _dot(a, w_ref[0].astype(bf16)) + b_ref[0]


def _ada_mod(c_all, w_ada, b_ada):
    rows = c_all.shape[0]
    tn = 1024
    return pl.pallas_call(
        _ada_kernel,
        grid=(DEPTH, 6 * D // tn),
        in_specs=[pl.BlockSpec((rows, D), lambda l, j: (0, 0)),
                  pl.BlockSpec((1, D, tn), lambda l, j: (l, 0, j)),
                  pl.BlockSpec((1, 1, tn), lambda l, j: (l, 0, j))],
        out_specs=pl.BlockSpec((1, rows, tn), lambda l, j: (l, 0, j)),
        out_shape=jax.ShapeDtypeStruct((DEPTH, rows, 6 * D), f32),
        compiler_params=_cparams(("arbitrary", "arbitrary"), 40),
        name="ada_mod",
    )(c_all, w_ada, b_ada.reshape(DEPTH, 1, 6 * D))


def _bucket_np(dist):
    n = np.maximum(dist, 0)
    exact = NUM_BUCKETS // 2
    nf = np.maximum(n, 1).astype(np.float32)
    large = exact + (np.log(nf / np.float32(exact)) / np.float32(math.log(MAX_DISTANCE / exact))
                     * np.float32(NUM_BUCKETS - exact)).astype(np.int32)
    return np.where(n < exact, n, np.clip(large, exact, NUM_BUCKETS - 1)).astype(np.int32)


def _prompt_buckets():
    j = np.arange(2 * WINDOW)
    dist = np.arange(WINDOW)[:, None] + WINDOW - j[None, :]
    band = (dist >= 0) & (dist < WINDOW)
    return np.where(band, _bucket_np(dist), -1).astype(np.int32)


def _sample_buckets():
    l = np.arange(SPAD)[:, None]
    j = np.arange(2 * WINDOW)[None, :]
    dist = WINDOW + l - j
    ok = (dist >= 0) & (dist < WINDOW) & (l < DEC_SEQ) & (j < WINDOW + DEC_SEQ)
    return np.where(ok, _bucket_np(dist), -1).astype(np.int32)


def _bias_kernel(tab_ref, bk_ref, o_ref):
    h = pl.program_id(0)
    bk = bk_ref[...]
    acc = jnp.zeros(bk.shape, f32)
    for i in range(NUM_BUCKETS):
        acc = jnp.where(bk == i, tab_ref[i, h], acc)
    o_ref[0] = jnp.where(bk < 0, NEG, acc)


def _bias_table(rel_bias, buckets):
    r, c = buckets.shape
    return pl.pallas_call(
        _bias_kernel,
        grid=(NH,),
        in_specs=[pl.BlockSpec(memory_space=pltpu.SMEM),
                  pl.BlockSpec((r, c), lambda h: (0, 0))],
        out_specs=pl.BlockSpec((1, r, c), lambda h: (h, 0, 0)),
        out_shape=jax.ShapeDtypeStruct((NH, r, c), f32),
        compiler_params=_cparams(("arbitrary",), 16),
        name="bias_table",
    )(rel_bias, jnp.asarray(buckets))


def _inproj_kernel(x_ref, g_ref, scp_ref, scs_ref, shp_ref, shs_ref, w_ref, o_ref, h_ref):
    i = pl.program_id(0)

    @pl.when(pl.program_id(1) == 0)
    def _():
        y = _rms(x_ref[...], g_ref[...])
        sc = _mod_rows(i >= NPT, scp_ref, scs_ref, TM)
        sh = _mod_rows(i >= NPT, shp_ref, shs_ref, TM)
        h_ref[...] = (y * (1.0 + sc) + sh).astype(bf16)

    o_ref[...] = _dot(h_ref[...], w_ref[0])


def _in_proj(l, x, norm_g, mod3, mod2, w_in_bf):
    scp, scs = _mod_specs(1, TPB, NPT)
    shp, shs = _mod_specs(0, TPB, NPT)
    return pl.pallas_call(
        _inproj_kernel,
        grid=(NT, IN_WIDTH // TN_IN),
        in_specs=[pl.BlockSpec((TM, D), lambda i, j: (i, 0)),
                  pl.BlockSpec((1, D), lambda i, j: (0, 0)),
                  scp, scs, shp, shs,
                  pl.BlockSpec((1, D, TN_IN), lambda i, j: (l, 0, j))],
        out_specs=pl.BlockSpec((TM, TN_IN), lambda i, j: (i, j)),
        out_shape=jax.ShapeDtypeStruct((T, IN_WIDTH), f32),
        scratch_shapes=[pltpu.VMEM((TM, D), bf16)],
        compiler_params=_cparams(("arbitrary", "arbitrary"), 56),
        name="in_proj",
    )(x, norm_g.reshape(1, D), mod3, mod2, mod3, mod2, w_in_bf)


def _hgrn_gates(z, lb):
    f = lb + (1.0 - lb) * jax.nn.sigmoid(z)
    g = jnp.log(jnp.maximum(f, TINY))
    k = (1.0 - lb) * jax.nn.sigmoid(-z)
    return g, k


def _bcast_block_row(b, period, row):
    c, w = b.shape
    b3 = b.reshape(c // period, period, w)
    return jnp.broadcast_to(b3[:, row:row + 1, :], b3.shape).reshape(c, w)


def _hgrn_level_table():
    t = np.arange(HC)[:, None]
    s = np.arange(HC)[None, :]
    lev = np.full((HC, HC), -1, np.int32)
    for li, w in enumerate(HGRN_WIDTHS + (2,)):
        m = (t // (2 * w) == s // (2 * w)) & ((t // w) % 2 == 1) & ((s // w) % 2 == 0)
        lev[m] = li
    lev[(t // 2 == s // 2) & (s <= t)] = len(HGRN_WIDTHS) + 1
    return lev


def _hgrn_prompt_kernel(q_ref, f_ref, v_ref, og_ref, lb_ref, ng_ref, lev_ref, o_ref, sfin_ref, st_ref):
    c = pl.program_id(1)

    @pl.when(c == 0)
    def _():
        st_ref[...] = jnp.zeros_like(st_ref)

    lb = lb_ref[...]
    g, k = _hgrn_gates(f_ref[...], lb)
    qh = _silu(q_ref[...])
    vb = v_ref[...].astype(bf16)
    r = lax.broadcasted_iota(i32, (HC, HC), 0)
    s = lax.broadcasted_iota(i32, (HC, HC), 1)
    tri = (r >= s).astype(bf16)
    g1, g2, g3 = _split3(g)
    b = _dot(tri, g1) + _dot(tri, g2) + _dot(tri, g3)
    lev = lev_ref[...]

    qs, ks = [], []
    for w in HGRN_WIDTHS:
        m = _bcast_block_row(b, 2 * w, w - 1)
        qs.append((qh * jnp.exp(jnp.minimum(b - m, 0.0))).astype(bf16))
        ks.append((k * jnp.exp(jnp.minimum(m - b, 0.0))).astype(bf16))
    row = lax.broadcasted_iota(i32, (HC, 1), 0)
    g_prev = pltpu.roll(g, 1, 0)
    g_next = pltpu.roll(g, HC - 1, 0)
    r4 = row & 3
    qs.append((qh * jnp.exp(jnp.where(r4 == 2, g, jnp.where(r4 == 3, g + g_prev, 0.0)))).astype(bf16))
    ks.append((k * jnp.exp(jnp.where(r4 == 0, g_next, 0.0))).astype(bf16))
    odd = (row & 1) == 1
    qs.append((qh * jnp.exp(jnp.where(odd, g, 0.0))).astype(bf16))
    ks.append((k * jnp.exp(jnp.where(odd, -g, 0.0))).astype(bf16))
    b_end = b[HC - 1:HC, :]
    q_in = (qh * jnp.exp(b)).astype(bf16)
    k_out = (k * jnp.exp(b_end - b)).astype(bf16)
    e_end = jnp.exp(b_end)
    gate = _silu(og_ref[...])
    ng = ng_ref[...]

    for h in range(HH):
        sl = slice(h * DK, (h + 1) * DK)
        a = jnp.zeros((HC, HC), f32)
        for li in range(len(qs)):
            a = jnp.where(lev == li, _dot_nt(qs[li][:, sl], ks[li][:, sl]), a)
        st = st_ref[h]
        o = _dot(a.astype(bf16), vb[:, sl]) + _dot_nt(q_in[:, sl], st.astype(bf16))
        st_new = st * e_end[:, sl] + _dot_tn(vb[:, sl], k_out[:, sl])
        st_ref[h] = st_new
        o_ref[:, sl] = (_rms(o, ng) * gate[:, sl]).astype(bf16)

        @pl.when(c == pl.num_programs(1) - 1)
        def _():
            sfin_ref[0, h] = st_new.T


def _hgrn_prompt(proj, lb, onorm_g):
    nc = SEQ // HC

    def slab(k):
        return pl.BlockSpec((HC, KW), lambda b, c: (b * nc + c, k))

    return pl.pallas_call(
        _hgrn_prompt_kernel,
        grid=(BATCH, nc),
        in_specs=[slab(OFF_HQ // KW), slab(OFF_HF // KW), slab(OFF_HV // KW), slab(OFF_HG // KW),
                  pl.BlockSpec((1, KW), lambda b, c: (0, 0)),
                  pl.BlockSpec((1, DV), lambda b, c: (0, 0)),
                  pl.BlockSpec((HC, HC), lambda b, c: (0, 0))],
        out_specs=[pl.BlockSpec((HC, KW), lambda b, c: (b * nc + c, 0)),
                   pl.BlockSpec((1, HH, DK, DV), lambda b, c: (b, 0, 0, 0))],
        out_shape=[jax.ShapeDtypeStruct((TP, KW), bf16),
                   jax.ShapeDtypeStruct((BATCH, HH, DK, DV), f32)],
        scratch_shapes=[pltpu.VMEM((HH, DV, DK), f32)],
        compiler_params=_cparams(("arbitrary", "arbitrary"), 48),
        name="hgrn_prompt",
    )(proj, proj, proj, proj, lb.reshape(1, KW), onorm_g.reshape(1, DV), jnp.asarray(_hgrn_level_table()))


def _hgrn_sample_kernel(q_ref, f_ref, v_ref, og_ref, lb_ref, ng_ref, s0_ref, o_ref, snew_ref,
                        qin_ref, kout_ref, eend_ref, oacc_ref, mt_ref, q16_ref, ostage_ref):
    step = pl.program_id(0)
    L, B = DEC_SEQ, DEC_BATCH

    @pl.when(step == 0)
    def _():
        lb = lb_ref[...]
        g, k = _hgrn_gates(f_ref[...], lb)
        qh = _silu(q_ref[...])
        v = v_ref[...]
        bs = []
        acc = None
        for t in range(L):
            gt = g[t * B:(t + 1) * B]
            acc = gt if acc is None else acc + gt
            bs.append(acc)
        b_end = bs[-1]
        eend_ref[...] = jnp.exp(b_end)
        for t in range(L):
            rt = slice(t * B, (t + 1) * B)
            qin_ref[rt, :] = qh[rt] * jnp.exp(bs[t])
            kout_ref[rt, :] = k[rt] * jnp.exp(b_end - bs[t])
            for h in range(HH):
                sl = slice(h * DK, (h + 1) * DK)
                o = jnp.zeros((B, DV), f32)
                for s_ in range(t + 1):
                    rs = slice(s_ * B, (s_ + 1) * B)
                    w = jnp.sum(qh[rt, sl] * k[rs, sl] * jnp.exp(bs[t][:, sl] - bs[s_][:, sl]),
                                axis=-1, keepdims=True)
                    o = o + w * v[rs, sl]
                oacc_ref[rt, sl] = o
        mt_ref[...] = jnp.zeros_like(mt_ref)
        q16_ref[...] = jnp.zeros_like(q16_ref)

    rows = [pl.ds(pl.multiple_of(t * B + step * SBB, SBB), SBB) for t in range(L)]
    q_t = [qin_ref[rows[t], :] for t in range(L)]
    k_t = [kout_ref[rows[t], :] for t in range(L)]
    v_t = [v_ref[rows[t], :] for t in range(L)]
    e_t = eend_ref[pl.ds(pl.multiple_of(step * SBB, SBB), SBB), :]
    for bi in range(SBB):
        for h in range(HH):
            sl = slice(h * DK, (h + 1) * DK)
            s0 = s0_ref[0, bi, h]
            for t in range(L):
                q16_ref[t:t + 1, :] = q_t[t][bi:bi + 1, sl]
            oi = _dot(q16_ref[...].astype(bf16), s0.astype(bf16))
            for t in range(L):
                ostage_ref[t, bi:bi + 1, sl] = oi[t:t + 1]
            mt_ref[0:1, :] = e_t[bi:bi + 1, sl]
            for t in range(L):
                mt_ref[t + 1:t + 2, :] = k_t[t][bi:bi + 1, sl]
            cols = mt_ref[...].T
            sn = s0 * cols[:, 0:1]
            for t in range(L):
                sn = sn + cols[:, t + 1:t + 2] * v_t[t][bi:bi + 1, sl]
            snew_ref[bi, h] = sn
    for t in range(L):
        oacc_ref[rows[t], :] += ostage_ref[t]

    @pl.when(step == pl.num_programs(0) - 1)
    def _():
        gate = _silu(og_ref[...])
        ng = ng_ref[...]
        for h in range(HH):
            sl = slice(h * DK, (h + 1) * DK)
            o_ref[:, sl] = (_rms(oacc_ref[:, sl], ng) * gate[:, sl]).astype(bf16)


def _hgrn_sample(l, proj, lb, onorm_g, state):
    def slab(k):
        return pl.BlockSpec((TS, KW), lambda s: (TP // TS, k))

    return pl.pallas_call(
        _hgrn_sample_kernel,
        grid=(DEC_BATCH // SBB,),
        in_specs=[slab(OFF_HQ // KW), slab(OFF_HF // KW), slab(OFF_HV // KW), slab(OFF_HG // KW),
                  pl.BlockSpec((1, KW), lambda s: (0, 0)),
                  pl.BlockSpec((1, DV), lambda s: (0, 0)),
                  pl.BlockSpec((1, SBB, HH, DK, DV), lambda s: (l, s, 0, 0, 0))],
        out_specs=[pl.BlockSpec((TS, KW), lambda s: (0, 0)),
                   pl.BlockSpec((SBB, HH, DK, DV), lambda s: (s, 0, 0, 0))],
        out_shape=[jax.ShapeDtypeStruct((TS, KW), bf16),
                   jax.ShapeDtypeStruct((DEC_BATCH, HH, DK, DV), f32)],
        scratch_shapes=[pltpu.VMEM((TS, KW), f32), pltpu.VMEM((TS, KW), f32),
                        pltpu.VMEM((DEC_BATCH, KW), f32), pltpu.VMEM((TS, KW), f32),
                        pltpu.VMEM((LANES, DK), f32), pltpu.VMEM((2 * SUBLANES, DK), f32),
                        pltpu.VMEM((DEC_SEQ, SBB, KW), f32)],
        compiler_params=_cparams(("arbitrary",), 56),
        name="hgrn_sample",
    )(proj, proj, proj, proj, lb.reshape(1, KW), onorm_g.reshape(1, DV), state)


def _head_rms(x, g, n):
    return jnp.concatenate([_rms(x[:, i * HD:(i + 1) * HD], g) for i in range(n)], axis=1)


def _sink_softmax_pv(s, sink_col, vb):
    m = jnp.maximum(jnp.max(s, axis=-1, keepdims=True), sink_col)
    p = jnp.exp(s - m)
    den = jnp.sum(p, axis=-1, keepdims=True) + jnp.exp(sink_col - m)
    return _dot(p.astype(bf16), vb) / den


def _attn_prompt_kernel(sink_ref, q_ref, kc_ref, kp_ref, vc_ref, vp_ref, qg_ref, kg_ref, bias_ref,
                        o_ref, kn_ref):
    i = pl.program_id(1)
    kc = _head_rms(kc_ref[...], kg_ref[...], NKV)
    kp = _head_rms(kp_ref[...], kg_ref[...], NKV)
    kn_ref[...] = kc
    kk = jnp.concatenate([kp, kc], axis=0).astype(bf16)
    vv = jnp.concatenate([vp_ref[...], vc_ref[...]], axis=0).astype(bf16)
    q = q_ref[...]
    col = lax.broadcasted_iota(i32, (GQ * WINDOW, 2 * WINDOW), 1)
    valid = (i > 0) | (col >= WINDOW)
    for n in range(NKV):
        qn = jnp.concatenate(
            [_rms(q[:, (n * GQ + g) * HD:(n * GQ + g + 1) * HD], qg_ref[...]) for g in range(GQ)], axis=0)
        s = _dot_nt(qn.astype(bf16), kk[:, n * HD:(n + 1) * HD]) * (HD ** -0.5)
        s = s + bias_ref[n * GQ:(n + 1) * GQ].reshape(GQ * WINDOW, 2 * WINDOW)
        s = jnp.where(valid, s, NEG)
        sink_col = jnp.concatenate(
            [jnp.full((WINDOW, 1), sink_ref[n * GQ + g], f32) for g in range(GQ)], axis=0)
        o = _sink_softmax_pv(s, sink_col, vv[:, n * HD:(n + 1) * HD])
        for g in range(GQ):
            hh = n * GQ + g
            o_ref[:, hh * HD:(hh + 1) * HD] = o[g * WINDOW:(g + 1) * WINDOW].astype(bf16)


def _attn_prompt(proj, sinks, q_norm_g, k_norm_g, bias):
    nb = SEQ // WINDOW
    kblk, vblk = OFF_AK // KVW, OFF_AV // KVW

    def cur(col):
        return lambda b, i: (b * nb + i, col)

    def prev(col):
        return lambda b, i: (b * nb + jnp.maximum(i - 1, 0), col)

    return pl.pallas_call(
        _attn_prompt_kernel,
        grid=(BATCH, nb),
        in_specs=[pl.BlockSpec(memory_space=pltpu.SMEM),
                  pl.BlockSpec((WINDOW, NH * HD), cur(OFF_AQ // (NH * HD))),
                  pl.BlockSpec((WINDOW, KVW), cur(kblk)),
                  pl.BlockSpec((WINDOW, KVW), prev(kblk)),
                  pl.BlockSpec((WINDOW, KVW), cur(vblk)),
                  pl.BlockSpec((WINDOW, KVW), prev(vblk)),
                  pl.BlockSpec((1, HD), lambda b, i: (0, 0)),
                  pl.BlockSpec((1, HD), lambda b, i: (0, 0)),
                  pl.BlockSpec((NH, WINDOW, 2 * WINDOW), lambda b, i: (0, 0, 0))],
        out_specs=[pl.BlockSpec((WINDOW, NH * HD), lambda b, i: (b * nb + i, 0)),
                   pl.BlockSpec((WINDOW, KVW), lambda b, i: (b * nb + i, 0))],
        out_shape=[jax.ShapeDtypeStruct((TP, NH * HD), bf16),
                   jax.ShapeDtypeStruct((TP, KVW), f32)],
        compiler_params=_cparams(("arbitrary", "arbitrary"), 32),
        name="attn_prompt",
    )(sinks, proj, proj, proj, proj, proj, q_norm_g.reshape(1, HD), k_norm_g.reshape(1, HD), bias)


def _attn_sample_kernel(sink_ref, q_ref, k_ref, v_ref, ck_ref, cv_ref, qg_ref, kg_ref, bias_ref,
                        o_ref, kw_ref, vw_ref):
    pad_rows = 2 * WINDOW - WINDOW - SPAD
    for bi in range(SBB):
        kn = _head_rms(k_ref[:, bi, :], kg_ref[...], NKV)
        vn = v_ref[:, bi, :]
        ck = ck_ref[0, bi]
        cv = cv_ref[0, bi]
        kw_ref[bi, 0:WINDOW - DEC_SEQ, :] = ck[DEC_SEQ:, :]
        kw_ref[bi, WINDOW - DEC_SEQ:WINDOW, :] = kn[0:DEC_SEQ]
        vw_ref[bi, 0:WINDOW - DEC_SEQ, :] = cv[DEC_SEQ:, :]
        vw_ref[bi, WINDOW - DEC_SEQ:WINDOW, :] = vn[0:DEC_SEQ]
        zpad = jnp.zeros((pad_rows, KVW), f32)
        kk = jnp.concatenate([ck, kn, zpad], axis=0).astype(bf16)
        vv = jnp.concatenate([cv, vn, zpad], axis=0).astype(bf16)
        q = q_ref[:, bi, :]
        for n in range(NKV):
            qn = jnp.concatenate(
                [_rms(q[:, (n * GQ + g) * HD:(n * GQ + g + 1) * HD], qg_ref[...]) for g in range(GQ)], axis=0)
            s = _dot_nt(qn.astype(bf16), kk[:, n * HD:(n + 1) * HD]) * (HD ** -0.5)
            s = s + bias_ref[n * GQ:(n + 1) * GQ].reshape(GQ * SPAD, 2 * WINDOW)
            sink_col = jnp.concatenate(
                [jnp.full((SPAD, 1), sink_ref[n * GQ + g], f32) for g in range(GQ)], axis=0)
            o = _sink_softmax_pv(s, sink_col, vv[:, n * HD:(n + 1) * HD])
            for g in range(GQ):
                hh = n * GQ + g
                o_ref[:, bi, hh * HD:(hh + 1) * HD] = o[g * SPAD:(g + 1) * SPAD].astype(bf16)


def _attn_sample(l, qkv_pad, cache_k, cache_v, sinks, q_norm_g, k_norm_g, bias):
    kblk = NH * HD // KVW
    return pl.pallas_call(
        _attn_sample_kernel,
        grid=(DEC_BATCH // SBB,),
        in_specs=[pl.BlockSpec(memory_space=pltpu.SMEM),
                  pl.BlockSpec((SPAD, SBB, NH * HD), lambda s: (0, s, 0)),
                  pl.BlockSpec((SPAD, SBB, KVW), lambda s: (0, s, kblk)),
                  pl.BlockSpec((SPAD, SBB, KVW), lambda s: (0, s, kblk + 1)),
                  pl.BlockSpec((1, SBB, WINDOW, KVW), lambda s: (l, s, 0, 0)),
                  pl.BlockSpec((1, SBB, WINDOW, KVW), lambda s: (l, s, 0, 0)),
                  pl.BlockSpec((1, HD), lambda s: (0, 0)),
                  pl.BlockSpec((1, HD), lambda s: (0, 0)),
                  pl.BlockSpec((NH, SPAD, 2 * WINDOW), lambda s: (0, 0, 0))],
        out_specs=[pl.BlockSpec((SPAD, SBB, NH * HD), lambda s: (0, s, 0)),
                   pl.BlockSpec((SBB, WINDOW, KVW), lambda s: (s, 0, 0)),
                   pl.BlockSpec((SBB, WINDOW, KVW), lambda s: (s, 0, 0))],
        out_shape=[jax.ShapeDtypeStruct((SPAD, DEC_BATCH, NH * HD), bf16),
                   jax.ShapeDtypeStruct((DEC_BATCH, WINDOW, KVW), f32),
                   jax.ShapeDtypeStruct((DEC_BATCH, WINDOW, KVW), f32)],
        compiler_params=_cparams(("arbitrary",), 32),
        name="attn_sample",
    )(sinks, qkv_pad, qkv_pad, qkv_pad, cache_k, cache_v, q_norm_g.reshape(1, HD), k_norm_g.reshape(1, HD), bias)


def _merge_kernel(ohp_ref, ohs_ref, oap_ref, oas_ref, ga_ref, gb_ref, x_ref, g1p_ref, g1s_ref,
                  wa_ref, wb_ref, wo_ref, o_ref, acc_ref):
    i, j = pl.program_id(0), pl.program_id(1)
    oh = jnp.where(i >= NPT, ohs_ref[...], ohp_ref[...])
    oa = jnp.where(i >= NPT, oas_ref[...], oap_ref[...])
    merged = (jax.nn.sigmoid(ga_ref[...]) * _dot(oh, wa_ref[0])
              + jax.nn.sigmoid(gb_ref[...]) * _dot(oa, wb_ref[0]))
    part = _dot(merged.astype(bf16), wo_ref[0])

    @pl.when(j == 0)
    def _():
        acc_ref[...] = part

    @pl.when(j > 0)
    def _():
        acc_ref[...] += part

    @pl.when(j == pl.num_programs(1) - 1)
    def _():
        g1 = _mod_rows(i >= NPT, g1p_ref, g1s_ref, TM)
        o_ref[...] = x_ref[...] + g1 * acc_ref[...]


def _merge(l, oh_p, oh_s, oa_p, oa_s, proj, x, mod3, mod2, wa_bf, wb_bf, wo_bf):
    assert TS == TM and KW == NH * HD
    g1p, g1s = _mod_specs(2, TPB, NPT)
    prompt_tile = pl.BlockSpec((TM, KW), lambda i, j: (jnp.minimum(i, NPT - 1), 0))
    sample_tile = pl.BlockSpec((TM, KW), lambda i, j: (0, 0))
    return pl.pallas_call(
        _merge_kernel,
        grid=(NT, D // TN_MG),
        in_specs=[prompt_tile, sample_tile, prompt_tile, sample_tile,
                  pl.BlockSpec((TM, TN_MG), lambda i, j: (i, OFF_GA // TN_MG + j)),
                  pl.BlockSpec((TM, TN_MG), lambda i, j: (i, OFF_GB // TN_MG + j)),
                  pl.BlockSpec((TM, D), lambda i, j: (i, 0)),
                  g1p, g1s,
                  pl.BlockSpec((1, KW, TN_MG), lambda i, j: (l, 0, j)),
                  pl.BlockSpec((1, NH * HD, TN_MG), lambda i, j: (l, 0, j)),
                  pl.BlockSpec((1, TN_MG, D), lambda i, j: (l, j, 0))],
        out_specs=pl.BlockSpec((TM, D), lambda i, j: (i, 0)),
        out_shape=jax.ShapeDtypeStruct((T, D), f32),
        scratch_shapes=[pltpu.VMEM((TM, D), f32)],
        compiler_params=_cparams(("arbitrary", "arbitrary"), 48),
        name="merge",
    )(oh_p, oh_s, oa_p, oa_s, proj, proj, x, mod3, mod2, wa_bf, wb_bf, wo_bf)


def _first_index(hit, idx, big, axis):
    return jnp.min(jnp.where(hit, idx, big), axis=axis, keepdims=True)


def _pack_bf16_pairs(h):
    lo = lax.bitcast_convert_type(h[:, :D // 2].astype(bf16).astype(f32), u32)
    hi = lax.bitcast_convert_type(h[:, D // 2:].astype(bf16).astype(f32), u32)
    return (hi & jnp.uint32(0xFFFF0000)) | (lo >> 16)


def _unpack_bf16_pairs(p):
    lo = lax.bitcast_convert_type(p << 16, f32).astype(bf16)
    hi = lax.bitcast_convert_type(p & jnp.uint32(0xFFFF0000), f32).astype(bf16)
    return lo, hi


def _router_kernel(x_ref, g_ref, scp_ref, scs_ref, shp_ref, shs_ref, wr_ref, rb_ref,
                   hp_ref, ids_ref, wts_ref, rank_ref, cnt_ref):
    i = pl.program_id(0)
    y = _rms(x_ref[...], g_ref[...])
    sc = _mod_rows(i >= NPT, scp_ref, scs_ref, TM)
    sh = _mod_rows(i >= NPT, shp_ref, shs_ref, TM)
    h = y * (1.0 + sc) + sh
    hp_ref[...] = _pack_bf16_pairs(h).reshape(TM, ROW_TILES_X, LANES)
    wr = wr_ref[0]
    w1 = wr.astype(bf16)
    w2 = (wr - w1.astype(f32)).astype(bf16)
    h1 = h.astype(bf16)
    h2 = (h - h1.astype(f32)).astype(bf16)
    logits = _dot_nt(w1, h1) + (_dot_nt(w1, h2) + _dot_nt(w2, h1))
    scores = jax.nn.sigmoid(logits)
    biased = scores + rb_ref[...]
    b3 = biased.reshape(NG, EPG, TM)
    e_in = lax.broadcasted_iota(i32, (NG, EPG, TM), 1).astype(f32)
    m1 = jnp.max(b3, axis=1, keepdims=True)
    first = _first_index(b3 == m1, e_in, float(EPG), 1)
    m2 = jnp.max(jnp.where(e_in == first, -jnp.inf, b3), axis=1, keepdims=True)
    gs = (m1 + m2).reshape(NG, TM)
    g_idx = lax.broadcasted_iota(i32, (NG, TM), 0).astype(f32)
    gsel = jnp.zeros((NG, TM), f32)
    for _ in range(TOPKG):
        m = jnp.max(gs, axis=0, keepdims=True)
        pick = g_idx == _first_index(gs == m, g_idx, float(NG), 0)
        gsel = jnp.where(pick, 1.0, gsel)
        gs = jnp.where(pick, -jnp.inf, gs)
    emask = jnp.broadcast_to(gsel.reshape(NG, 1, TM), (NG, EPG, TM)).reshape(NE, TM)
    cur = jnp.where(emask > 0.5, biased, NEG)
    e_idx = lax.broadcasted_iota(i32, (NE, TM), 0).astype(f32)
    ids, ws, picks = [], [], []
    for _ in range(TOPK):
        m = jnp.max(cur, axis=0, keepdims=True)
        first = _first_index(cur == m, e_idx, float(NE), 0)
        pick = e_idx == first
        ids.append(first)
        picks.append(pick)
        ws.append(jnp.sum(jnp.where(pick, scores, 0.0), axis=0, keepdims=True))
        cur = jnp.where(pick, -jnp.inf, cur)
    w = jnp.concatenate(ws, axis=0)
    ids_ref[...] = jnp.concatenate(ids, axis=0).astype(i32)
    wts_ref[...] = w / jnp.sum(w, axis=0, keepdims=True) * ROUTED_SCALE

    @pl.when(i == 0)
    def _():
        cnt_ref[...] = jnp.zeros_like(cnt_ref)

    onehot = jnp.zeros((NE, TM), f32)
    for pick in picks:
        onehot = jnp.where(pick, 1.0, onehot)
    before = (lax.broadcasted_iota(i32, (TM, TM), 0) < lax.broadcasted_iota(i32, (TM, TM), 1)).astype(bf16)
    base = cnt_ref[:, 0:1] + _dot(onehot.astype(bf16), before)
    rank_ref[...] = jnp.concatenate(
        [jnp.sum(jnp.where(pick, base, 0.0), axis=0, keepdims=True) for pick in picks], axis=0).astype(i32)
    cnt_ref[...] += jnp.sum(onehot, axis=1, keepdims=True)


def _router(l, x1, norm_g, mod3, mod2, w_router_t, router_bias):
    scp, scs = _mod_specs(4, TPB, NPT)
    shp, shs = _mod_specs(3, TPB, NPT)
    return pl.pallas_call(
        _router_kernel,
        grid=(NT,),
        in_specs=[pl.BlockSpec((TM, D), lambda i: (i, 0)),
                  pl.BlockSpec((1, D), lambda i: (0, 0)),
                  scp, scs, shp, shs,
                  pl.BlockSpec((1, NE, D), lambda i: (l, 0, 0)),
                  pl.BlockSpec((NE, 1), lambda i: (0, 0))],
        out_specs=[pl.BlockSpec((TM, ROW_TILES_X, LANES), lambda i: (i, 0, 0)),
                   pl.BlockSpec((TOPK, TM), lambda i: (0, i)),
                   pl.BlockSpec((TOPK, TM), lambda i: (0, i)),
                   pl.BlockSpec((TOPK, TM), lambda i: (0, i)),
                   pl.BlockSpec((NE, LANES), lambda i: (0, 0))],
        out_shape=[jax.ShapeDtypeStruct((T, ROW_TILES_X, LANES), u32),
                   jax.ShapeDtypeStruct((TOPK, T), i32),
                   jax.ShapeDtypeStruct((TOPK, T), f32),
                   jax.ShapeDtypeStruct((TOPK, T), i32),
                   jax.ShapeDtypeStruct((NE, LANES), f32)],
        compiler_params=_cparams(("arbitrary",), 48),
        name="router",
    )(x1, norm_g.reshape(1, D), mod3, mod2, mod3, mod2, w_router_t, router_bias.reshape(NE, 1))


def _block_tables(cnt):
    counts = cnt[:, 0].astype(i32)
    padded = (counts + BLK - 1) // BLK * BLK
    pend = jnp.cumsum(padded)
    pstart = pend - padded
    blk_start = jnp.arange(NBLK, dtype=i32) * BLK
    block_e = jnp.minimum(jnp.sum(blk_start[:, None] >= pend[None, :], axis=1), NE - 1).astype(i32)
    n_used = (pend[-1] // BLK).astype(i32).reshape(1)
    after = (pend // BLK).astype(i32)
    next_of_expert = jnp.where(after < n_used[0], block_e[jnp.minimum(after, NBLK - 1)], -1)
    next_e = next_of_expert[block_e].astype(i32)
    return (block_e, n_used, next_e, pstart.astype(i32), (pstart + counts).astype(i32),
            (padded - counts).astype(i32))


def _pos_kernel(ps_ref, ids_ref, rank_ref, pos_ref):
    ids = ids_ref[...]
    acc = rank_ref[...]
    for e in range(NE):
        acc = acc + jnp.where(ids == e, ps_ref[e], 0)
    pos_ref[...] = acc


def _slot_positions(pstart, ids_t, rank_t):
    return pl.pallas_call(
        _pos_kernel,
        grid=(1,),
        in_specs=[pl.BlockSpec(memory_space=pltpu.SMEM),
                  pl.BlockSpec((TOPK, T), lambda i: (0, 0)),
                  pl.BlockSpec((TOPK, T), lambda i: (0, 0))],
        out_specs=pl.BlockSpec((TOPK, T), lambda i: (0, 0)),
        out_shape=jax.ShapeDtypeStruct((TOPK, T), i32),
        compiler_params=_cparams(("arbitrary",), 16),
        name="slot_positions",
    )(pstart, ids_t, rank_t)


def _dispatch_kernel(pad0_ref, padn_ref, nu_ref, pos_ref, hp_hbm, xs_hbm, zero_ref, sem):
    i = pl.program_id(0)

    def tok_copy(k, j, u):
        r = j * SUBLANES + u
        return pltpu.make_async_copy(hp_hbm.at[i * TMF + r], xs_hbm.at[pos_ref[0, 0, k * TMF + r]], sem)

    def pad_copy(e, r):
        return pltpu.make_async_copy(zero_ref.at[0], xs_hbm.at[pad0_ref[e] + r], sem)

    def tail_copy(b):
        return pltpu.make_async_copy(zero_ref, xs_hbm.at[pl.ds(b * BLK, BLK)], sem)

    def for_tokens(fn):
        for k in range(TOPK):
            def body(j, c):
                for u in range(SUBLANES):
                    fn(tok_copy(k, j, u), u)
                return c
            lax.fori_loop(0, TMF // SUBLANES, body, 0)

    def for_pads(fn):
        def per_expert(e, c):
            def body(r, c2):
                fn(pad_copy(e, r), 0)
                return c2
            return lax.fori_loop(0, padn_ref[e], body, c)
        lax.fori_loop(0, NE, per_expert, 0)

        def per_block(b, c):
            fn(tail_copy(b), 0)
            return c
        lax.fori_loop(nu_ref[0], NBLK, per_block, 0)

    @pl.when(i == 0)
    def _():
        zero_ref[...] = jnp.zeros_like(zero_ref)
        for_pads(lambda cp, u: cp.start(priority=u % 2))

    for_tokens(lambda cp, u: cp.start(priority=u % 2))
    for_tokens(lambda cp, u: cp.wait())

    @pl.when(i == 0)
    def _():
        for_pads(lambda cp, u: cp.wait())


def _dispatch(pad_start, pad_cnt, n_used, pos_tiles, hp):
    grid_spec = pltpu.PrefetchScalarGridSpec(
        num_scalar_prefetch=3,
        grid=(T // TMF,),
        in_specs=[pl.BlockSpec((1, 1, TOPK * TMF), lambda i, p0, pn, nu: (i, 0, 0), memory_space=pltpu.SMEM),
                  pl.BlockSpec(memory_space=pl.ANY)],
        out_specs=pl.BlockSpec(memory_space=pl.ANY),
        scratch_shapes=[pltpu.VMEM((BLK, ROW_TILES_X, LANES), u32), pltpu.SemaphoreType.DMA(())],
    )
    return pl.pallas_call(
        _dispatch_kernel,
        grid_spec=grid_spec,
        out_shape=jax.ShapeDtypeStruct((NROW, ROW_TILES_X, LANES), u32),
        compiler_params=_cparams(("arbitrary",), 16),
        name="dispatch",
    )(pad_start, pad_cnt, n_used, pos_tiles, hp)


def _expert_kernel(l, be_ref, nu_ref, nx_ref, x_ref, wg_hbm, wu_hbm, wd_hbm, y_ref,
                   wg_f, wu_f, wd_f, wsem, ord_ref, wgb, wub, wdb):
    i = pl.program_id(0)

    def weight_copies(e, s):
        return (pltpu.make_async_copy(wg_hbm.at[l, e], wg_f.at[s], wsem.at[s]),
                pltpu.make_async_copy(wu_hbm.at[l, e], wu_f.at[s], wsem.at[s]),
                pltpu.make_async_copy(wd_hbm.at[l, e], wd_f.at[s], wsem.at[s]))

    @pl.when(i == 0)
    def _():
        ord_ref[0] = 0
        for cp in weight_copies(be_ref[0], 0):
            cp.start()

    @pl.when(i >= nu_ref[0])
    def _():
        y_ref[...] = jnp.zeros_like(y_ref)

    @pl.when(i < nu_ref[0])
    def _():
        @pl.when((i == 0) | (be_ref[i] != be_ref[jnp.maximum(i - 1, 0)]))
        def _():
            s = ord_ref[0] % 2
            for cp in weight_copies(be_ref[i], s):
                cp.wait()
            wgb[...] = wg_f[s].astype(bf16)
            wub[...] = wu_f[s].astype(bf16)
            wdb[...] = wd_f[s].astype(bf16)

            @pl.when(nx_ref[i] >= 0)
            def _():
                for cp in weight_copies(nx_ref[i], 1 - s):
                    cp.start()

            ord_ref[0] = ord_ref[0] + 1

        lo, hi = _unpack_bf16_pairs(x_ref[...].reshape(BLK, D // 2))
        half = D // 2
        g = _dot(lo, wgb[0:half, :]) + _dot(hi, wgb[half:D, :])
        u = _dot(lo, wub[0:half, :]) + _dot(hi, wub[half:D, :])
        y = _dot((_silu(g) * u).astype(bf16), wdb[...])
        y_ref[...] = _pack_bf16_pairs(y).reshape(BLK, ROW_TILES_X, LANES)


def _experts(l, xs, block_e, n_used, next_e, w_gate, w_up, w_down):
    grid_spec = pltpu.PrefetchScalarGridSpec(
        num_scalar_prefetch=3,
        grid=(NBLK,),
        in_specs=[pl.BlockSpec((BLK, ROW_TILES_X, LANES), lambda i, be, nu, nx: (jnp.minimum(i, nu[0] - 1), 0, 0)),
                  pl.BlockSpec(memory_space=pl.ANY),
                  pl.BlockSpec(memory_space=pl.ANY),
                  pl.BlockSpec(memory_space=pl.ANY)],
        out_specs=pl.BlockSpec((BLK, ROW_TILES_X, LANES), lambda i, be, nu, nx: (i, 0, 0)),
        scratch_shapes=[pltpu.VMEM((2, D, DE), f32), pltpu.VMEM((2, D, DE), f32), pltpu.VMEM((2, DE, D), f32),
                        pltpu.SemaphoreType.DMA((2,)), pltpu.SMEM((1,), i32),
                        pltpu.VMEM((D, DE), bf16), pltpu.VMEM((D, DE), bf16), pltpu.VMEM((DE, D), bf16)],
    )
    return pl.pallas_call(
        functools.partial(_expert_kernel, l),
        grid_spec=grid_spec,
        out_shape=jax.ShapeDtypeStruct((NROW, ROW_TILES_X, LANES), u32),
        compiler_params=_cparams(("arbitrary",), 56),
        name="experts",
    )(block_e, n_used, next_e, xs, w_gate, w_up, w_down)


def _final_kernel(pos_ref, posn_ref, x_ref, hp_ref, w_ref, g2p_ref, g2s_ref, sg_ref, su_ref, sd_ref,
                  y_hbm, o_ref, ybuf, sem):
    i = pl.program_id(0)
    n = pl.num_programs(0)
    slot = i % 2

    def row_copy(pref, k, j, u, s):
        r = j * SUBLANES + u
        return pltpu.make_async_copy(y_hbm.at[pref[0, 0, k * TMF + r]], ybuf.at[s, k, r], sem.at[s])

    def for_rows(pref, s, fn):
        for k in range(TOPK):
            def body(j, c):
                for u in range(SUBLANES):
                    fn(row_copy(pref, k, j, u, s), u)
                return c
            lax.fori_loop(0, TMF // SUBLANES, body, 0)

    @pl.when(i == 0)
    def _():
        for_rows(pos_ref, 0, lambda cp, u: cp.start(priority=u % 2))

    @pl.when(i + 1 < n)
    def _():
        for_rows(posn_ref, 1 - slot, lambda cp, u: cp.start(priority=u % 2))

    lo, hi = _unpack_bf16_pairs(hp_ref[...].reshape(TMF, D // 2))
    hb = jnp.concatenate([lo, hi], axis=1)
    shared = _dot((_silu(_dot(hb, sg_ref[0])) * _dot(hb, su_ref[0])).astype(bf16), sd_ref[0])

    for_rows(pos_ref, slot, lambda cp, u: cp.wait())

    w = w_ref[...]
    r_lo = jnp.zeros((TMF, D // 2), f32)
    r_hi = jnp.zeros((TMF, D // 2), f32)
    for k in range(TOPK):
        p = ybuf[slot, k].reshape(TMF, D // 2)
        r_lo = r_lo + w[:, k:k + 1] * lax.bitcast_convert_type(p << 16, f32)
        r_hi = r_hi + w[:, k:k + 1] * lax.bitcast_convert_type(p & jnp.uint32(0xFFFF0000), f32)
    routed = jnp.concatenate([r_lo, r_hi], axis=1)
    g2 = _mod_rows(i >= TP // TMF, g2p_ref, g2s_ref, TMF)
    o_ref[...] = x_ref[...] + g2 * (shared + routed)


def _final(l, x1, hp, wts, pos_tiles, mod3, mod2, sg_bf, su_bf, sd_bf, yb):
    g2p, g2s = _mod_specs(5, SEQ // TMF, TP // TMF)
    ntile = T // TMF
    return pl.pallas_call(
        _final_kernel,
        grid=(ntile,),
        in_specs=[pl.BlockSpec((1, 1, TOPK * TMF), lambda i: (i, 0, 0), memory_space=pltpu.SMEM),
                  pl.BlockSpec((1, 1, TOPK * TMF), lambda i: (jnp.minimum(i + 1, ntile - 1), 0, 0),
                               memory_space=pltpu.SMEM),
                  pl.BlockSpec((TMF, D), lambda i: (i, 0)),
                  pl.BlockSpec((TMF, ROW_TILES_X, LANES), lambda i: (i, 0, 0)),
                  pl.BlockSpec((TMF, TOPK), lambda i: (i, 0)),
                  g2p, g2s,
                  pl.BlockSpec((1, D, DE), lambda i: (l, 0, 0)),
                  pl.BlockSpec((1, D, DE), lambda i: (l, 0, 0)),
                  pl.BlockSpec((1, DE, D), lambda i: (l, 0, 0)),
                  pl.BlockSpec(memory_space=pl.ANY)],
        out_specs=pl.BlockSpec((TMF, D), lambda i: (i, 0)),
        out_shape=jax.ShapeDtypeStruct((T, D), f32),
        scratch_shapes=[pltpu.VMEM((2, TOPK, TMF, ROW_TILES_X, LANES), u32),
                        pltpu.SemaphoreType.DMA((2,))],
        compiler_params=_cparams(("arbitrary",), 56),
        name="moe_combine",
    )(pos_tiles, pos_tiles, x1, hp, wts, mod3, mod2, sg_bf, su_bf, sd_bf, yb)


def _layer(l, x, mod, lb, state, cache_k, cache_v, bias_p, bias_s, norm1_g, norm2_g, w_in_bf, onorm_g,
           q_norm_g, k_norm_g, sinks, wa_bf, wb_bf, wo_bf, w_router_t, router_bias,
           w_eg, w_eu, w_ed, sg_bf, su_bf, sd_bf):
    mod3 = mod.reshape(mod.shape[0], 1, 6 * D)
    proj = _in_proj(l, x, norm1_g, mod3, mod, w_in_bf)

    oh_p, s_p = _hgrn_prompt(proj, lb, onorm_g)
    oh_s, s_s = _hgrn_sample(l, proj, lb, onorm_g, state)

    oa_p, kn_p = _attn_prompt(proj, sinks, q_norm_g, k_norm_g, bias_p)
    qkv_pad = jnp.pad(proj[TP:, OFF_AQ:OFF_GA].reshape(DEC_SEQ, DEC_BATCH, OFF_GA - OFF_AQ),
                      ((0, SPAD - DEC_SEQ), (0, 0), (0, 0)))
    oa_s, kw_s, vw_s = _attn_sample(l, qkv_pad, cache_k, cache_v, sinks, q_norm_g, k_norm_g, bias_s)

    x1 = _merge(l, oh_p, oh_s, oa_p, oa_s[:DEC_SEQ].reshape(TS, NH * HD), proj, x, mod3, mod, wa_bf, wb_bf, wo_bf)

    hp, ids_t, wts_t, rank_t, cnt = _router(l, x1, norm2_g, mod3, mod, w_router_t, router_bias)
    block_e, n_used, next_e, pstart, pad_start, pad_cnt = _block_tables(cnt)
    pos = _slot_positions(pstart, ids_t, rank_t)
    pos_tiles = pos.reshape(TOPK, T // TMF, TMF).transpose(1, 0, 2).reshape(T // TMF, 1, TOPK * TMF)
    xs = _dispatch(pad_start, pad_cnt, n_used, pos_tiles, hp)
    yb = _experts(l, xs, block_e, n_used, next_e, w_eg, w_eu, w_ed)
    x2 = _final(l, x1, hp, wts_t.T, pos_tiles, mod3, mod, sg_bf, su_bf, sd_bf, yb)

    k_p = kn_p.reshape(BATCH, SEQ, NKV, HD)[:, SEQ - WINDOW:]
    v_p = proj[:TP, OFF_AV:OFF_GA].reshape(BATCH, SEQ, NKV, HD)[:, SEQ - WINDOW:]
    return (x2, s_p, s_s, k_p, v_p, kw_s.reshape(DEC_BATCH, WINDOW, NKV, HD),
            vw_s.reshape(DEC_BATCH, WINDOW, NKV, HD))


def kernel(x_prompt, x_sample, state_hgrn, cache_k_win, cache_v_win, c_prompt, c_sample, norm1_g, norm2_g,
           w_ada, b_ada, w_in, hgrn_lb_logits, hgrn_onorm_g, q_norm_g, k_norm_g, attn_sinks, rel_bias,
           w_branch_a, w_branch_b, w_out, w_router, router_bias, w_exp_gate, w_exp_up, w_exp_down,
           w_sh_gate, w_sh_up, w_sh_down):
    p_lb = jax.nn.softmax(hgrn_lb_logits.astype(f32), axis=0)
    lower_bounds = jnp.clip(jnp.cumsum(p_lb, axis=0) - p_lb[0], 0.0, 1.0 - 1e-6)

    c_all = jnp.concatenate([c_sample, c_prompt, jnp.zeros((SUBLANES - BATCH, D), f32)], axis=0)
    mod = _ada_mod(c_all, w_ada, b_ada)
    bias_p = _bias_table(rel_bias, _prompt_buckets())
    bias_s = _bias_table(rel_bias, _sample_buckets())

    x = jnp.concatenate([x_prompt.reshape(TP, D), x_sample.transpose(1, 0, 2).reshape(TS, D)], axis=0)
    cache_k = cache_k_win.reshape(DEPTH, DEC_BATCH, WINDOW, KVW)
    cache_v = cache_v_win.reshape(DEPTH, DEC_BATCH, WINDOW, KVW)
    w_in_bf, wa_bf, wb_bf, wo_bf = (w.astype(bf16) for w in (w_in, w_branch_a, w_branch_b, w_out))
    sg_bf, su_bf, sd_bf = (w.astype(bf16) for w in (w_sh_gate, w_sh_up, w_sh_down))
    w_router_t = w_router.transpose(0, 2, 1)
    sp_l, ss_l, kp_l, vp_l, ks_l, vs_l = [], [], [], [], [], []
    for l in range(DEPTH):
        x, s_p, s_s, k_p, v_p, k_s, v_s = _layer(
            l, x, mod[l], lower_bounds[l], state_hgrn, cache_k, cache_v, bias_p, bias_s,
            norm1_g[l], norm2_g[l], w_in_bf, hgrn_onorm_g[l], q_norm_g[l], k_norm_g[l], attn_sinks[l],
            wa_bf, wb_bf, wo_bf, w_router_t, router_bias[l],
            w_exp_gate, w_exp_up, w_exp_down, sg_bf, su_bf, sd_bf)
        sp_l.append(s_p); ss_l.append(s_s); kp_l.append(k_p); vp_l.append(v_p); ks_l.append(k_s); vs_l.append(v_s)

    y_prompt = x[:TP].reshape(BATCH, SEQ, D)
    y_sample = x[TP:].reshape(DEC_SEQ, DEC_BATCH, D).transpose(1, 0, 2)
    return (y_prompt, y_sample, jnp.stack(sp_l), jnp.stack(ss_l), jnp.stack(kp_l), jnp.stack(vp_l),
            jnp.stack(ks_l), jnp.stack(vs_l))
```
